```python
import functools
import jax, jax.numpy as jnp
from jax import lax
import numpy as np

D_MODEL = 2048
BATCH = 1
SEQ = 8192
DEPTH = 2
DEC_BATCH = 128
DEC_SEQ = 1
PAST_LEN = 2048
PAGE_SIZE = 128

HEAD_DIM = 128
HEADS_PER_GROUP = 4
DIL_GROUPS = ((128, 1), (512, 4), (2048, 16))
N_GROUPS = len(DIL_GROUPS)
ATTN_HEADS = N_GROUPS * HEADS_PER_GROUP
ATTN_WIDTH = ATTN_HEADS * HEAD_DIM
SLOT_WIDTH = HEADS_PER_GROUP * HEAD_DIM
ROT_DIM = HEAD_DIM // 4
ROPE_THETA = 500000.0
CONV_WIDTH = 3 * D_MODEL // 4
CONV_K = 31
N_MEM = 256
X_HEADS = 4
X_HEAD_DIM = D_MODEL // X_HEADS
D_FF = 11 * D_MODEL // 4
FFN_K = 3
Q_BLOCK = 128
N_IN = 3 * ATTN_WIDTH + 2 * CONV_WIDTH + 2 * D_MODEL
LN_EPS = 1e-5
DN_ALPHA = (2 * DEPTH) ** 0.25
DN_BETA = (8 * DEPTH) ** -0.25

kernel_name = 'hybrid_dilated_conformer_decoder_step'


def layer_norm(x, g, b):
    xf = x.astype(jnp.float32)
    mu = jnp.mean(xf, axis=-1, keepdims=True)
    xc = xf - mu
    var = jnp.mean(xc * xc, axis=-1, keepdims=True)
    return (xc * lax.rsqrt(var + LN_EPS) * g + b).astype(x.dtype)


def rope_partial(x, pos):
    inv = ROPE_THETA ** (-jnp.arange(0, ROT_DIM, 2, dtype=jnp.float32) / ROT_DIM)
    ang = pos.astype(jnp.float32)[:, None] * inv[None, :]
    cos = jnp.cos(ang)[None, :, None, :]
    sin = jnp.sin(ang)[None, :, None, :]
    half = ROT_DIM // 2
    x1 = x[..., :half].astype(jnp.float32)
    x2 = x[..., half:ROT_DIM].astype(jnp.float32)
    rot = jnp.concatenate([x1 * cos - x2 * sin, x2 * cos + x1 * sin], axis=-1).astype(x.dtype)
    return jnp.concatenate([rot, x[..., ROT_DIM:]], axis=-1)


def causal_dwconv(x_ext, w, b):
    c = x_ext.shape[-1]
    y = lax.conv_general_dilated(x_ext, w[:, None, :].astype(x_ext.dtype), window_strides=(1,),
                                 padding='VALID', dimension_numbers=('NWC', 'WIO', 'NWC'),
                                 feature_group_count=c)
    return y + b


def group_heads(kv, g):
    return kv[:, :, :, g * HEADS_PER_GROUP:(g + 1) * HEADS_PER_GROUP]


def group_attend(q, kv_src, q_idx, valid_from, dil, n_keys):
    key_idx = q_idx[:, None] - dil * jnp.arange(n_keys)[None, :]
    valid = key_idx >= valid_from
    kv = kv_src[:, jnp.maximum(key_idx, 0)]
    k, v = kv[:, :, :, 0], kv[:, :, :, 1]
    s = jnp.einsum('bthd,btnhd->bhtn', q, k).astype(jnp.float32) * (HEAD_DIM ** -0.5)
    s = jnp.where(valid[None, None], s, -jnp.inf)
    lse = jax.nn.logsumexp(s, axis=-1)
    p = jnp.exp(s - lse[..., None])
    o = jnp.einsum('bhtn,btnhd->bthd', p.astype(v.dtype), v)
    return o, lse


def dilated_mixture(q, srcs, q_idx, valid_from):
    outs, lses = [], []
    for g, (win, dil) in enumerate(DIL_GROUPS):
        qg = q[:, :, g * HEADS_PER_GROUP:(g + 1) * HEADS_PER_GROUP]
        o, l = group_attend(qg, srcs[g], q_idx[g], valid_from[g], dil, win // dil + 1)
        outs.append(o)
        lses.append(l)
    wts = jax.nn.softmax(jnp.stack(lses), axis=0)
    wts = jnp.swapaxes(wts, 2, 3)[..., None]
    o = jnp.sum(jnp.stack(outs).astype(jnp.float32) * wts, axis=0).astype(q.dtype)
    bsz, t = q.shape[0], q.shape[1]
    return o.reshape(bsz, t, SLOT_WIDTH)


def prompt_attention(q, kv):
    bsz, seq = q.shape[0], q.shape[1]
    pads = [jnp.pad(group_heads(kv, g), ((0, 0), (win, 0), (0, 0), (0, 0), (0, 0)))
            for g, (win, _) in enumerate(DIL_GROUPS)]

    def block(t0):
        qb = lax.dynamic_slice_in_dim(q, t0, Q_BLOCK, axis=1)
        srcs = [lax.dynamic_slice_in_dim(p, t0, win + Q_BLOCK, axis=1)
                for p, (win, _) in zip(pads, DIL_GROUPS)]
        q_idx = [win + jnp.arange(Q_BLOCK) for win, _ in DIL_GROUPS]
        valid_from = [win - t0 for win, _ in DIL_GROUPS]
        return dilated_mixture(qb, srcs, q_idx, valid_from)

    starts = jnp.arange(seq // Q_BLOCK, dtype=jnp.int32) * Q_BLOCK
    out = lax.map(block, starts)
    return jnp.swapaxes(out, 0, 1).reshape(bsz, seq, SLOT_WIDTH)


def sample_attention(caches, q, kv):
    t = q.shape[1]
    srcs, q_idx, valid_from = [], [], []
    for g, cache in enumerate(caches):
        n_past = cache.shape[1]
        srcs.append(jnp.concatenate([cache.astype(kv.dtype), group_heads(kv, g)], axis=1))
        q_idx.append(n_past + jnp.arange(t))
        valid_from.append(0)
    return dilated_mixture(q, srcs, q_idx, valid_from)


def split_in_proj(x, w_in, b_gate):
    bsz, t, _ = x.shape
    h = x @ w_in
    q, k, v, a, bg, gates = jnp.split(h, [ATTN_WIDTH, 2 * ATTN_WIDTH, 3 * ATTN_WIDTH,
                                          3 * ATTN_WIDTH + CONV_WIDTH,
                                          3 * ATTN_WIDTH + 2 * CONV_WIDTH], axis=-1)
    shape = (bsz, t, ATTN_HEADS, HEAD_DIM)
    u = a * jax.nn.sigmoid(bg)
    g_attn, g_conv = jnp.split(jax.nn.sigmoid(gates + b_gate), 2, axis=-1)
    return q.reshape(shape), k.reshape(shape), v.reshape(shape), u, g_attn, g_conv


def cross_attend(x, mem_kv, w_xq, w_xo):
    bsz, t, _ = x.shape
    q = (x @ w_xq).reshape(bsz, t, X_HEADS, X_HEAD_DIM)
    k, v = mem_kv[:, :, 0].astype(x.dtype), mem_kv[:, :, 1].astype(x.dtype)
    s = jnp.einsum('bthd,bmhd->bhtm', q, k).astype(jnp.float32) * (X_HEAD_DIM ** -0.5)
    p = jax.nn.softmax(s, axis=-1)
    o = jnp.einsum('bhtm,bmhd->bthd', p.astype(v.dtype), v)
    return o.reshape(bsz, t, D_MODEL) @ w_xo


def decoder_layer(x, pos, attend, conv_hist, ffn_hist, mem_kv,
                  w_in, b_gate, w_attn_out, conv_w, conv_b, conv_ln_g, conv_ln_b, w_conv_out,
                  w_mix_out, ln1_g, ln1_b, w_xq, w_xo, ln2_g, ln2_b,
                  w_up, ffn_conv_w, ffn_conv_b, w_down, ln3_g, ln3_b):
    q, k, v, u, g_attn, g_conv = split_in_proj(x, w_in, b_gate)
    q = rope_partial(q, pos)
    k = rope_partial(k, pos)
    kv = jnp.stack([k, v], axis=2)
    a = attend(q, kv) @ w_attn_out
    u_ext = jnp.concatenate([conv_hist.astype(u.dtype), u], axis=1)
    c = causal_dwconv(u_ext, conv_w, conv_b)
    c = jax.nn.silu(layer_norm(c, conv_ln_g, conv_ln_b)) @ w_conv_out
    mix = (g_attn * a + g_conv * c) @ w_mix_out
    x = layer_norm(DN_ALPHA * x + mix, ln1_g, ln1_b)
    x = layer_norm(DN_ALPHA * x + cross_attend(x, mem_kv, w_xq, w_xo), ln2_g, ln2_b)
    h = x @ w_up
    h_ext = jnp.concatenate([ffn_hist.astype(h.dtype), h], axis=1)
    gate, val = jnp.split(causal_dwconv(h_ext, ffn_conv_w, ffn_conv_b), 2, axis=-1)
    f = (jax.nn.silu(gate) * val) @ w_down
    x = layer_norm(DN_ALPHA * x + f, ln3_g, ln3_b)
    return x, kv, u_ext[:, -(CONV_K - 1):], h_ext[:, -(FFN_K - 1):]


def setup_inputs(seed: int = 0) -> dict:
    key = jax.random.key(seed)
    ks = iter(jax.random.split(key, 40))
    f32 = jnp.float32

    def nrm(shape, scale=1.0):
        return scale * jax.random.normal(next(ks), shape, f32)

    win_lens = [min(win, PAST_LEN) for win, _ in DIL_GROUPS]
    in_scale = jnp.concatenate([jnp.ones((2 * ATTN_WIDTH,), f32),
                                jnp.full((ATTN_WIDTH,), DN_BETA, f32),
                                jnp.ones((2 * CONV_WIDTH + 2 * D_MODEL,), f32)])
    xkv_scale = jnp.concatenate([jnp.ones((D_MODEL,), f32), jnp.full((D_MODEL,), DN_BETA, f32)])
    d = D_MODEL
    return {
        'x_prompt': nrm((BATCH, SEQ, d)),
        'x_sample': nrm((DEC_BATCH, DEC_SEQ, d)),
        'cache_win1_kv': nrm((DEPTH, DEC_BATCH, win_lens[0], 2, HEADS_PER_GROUP, HEAD_DIM)),
        'cache_win2_kv': nrm((DEPTH, DEC_BATCH, win_lens[1], 2, HEADS_PER_GROUP, HEAD_DIM)),
        'cache_win3_kv': nrm((DEPTH, DEC_BATCH, win_lens[2], 2, HEADS_PER_GROUP, HEAD_DIM)),
        'state_conv': nrm((DEPTH, DEC_BATCH, CONV_K - 1, CONV_WIDTH), 0.5),
        'state_ffn_conv': nrm((DEPTH, DEC_BATCH, FFN_K - 1, 2 * D_FF), 0.5),
        'cache_mem_kv': nrm((DEPTH, DEC_BATCH, N_MEM, 2, X_HEADS, X_HEAD_DIM)),
        'mem_prompt': nrm((BATCH, N_MEM, d)),
        'w_in': nrm((DEPTH, d, N_IN), d ** -0.5) * in_scale,
        'b_gate': nrm((DEPTH, 2 * d), 0.02),
        'w_attn_out': nrm((DEPTH, SLOT_WIDTH, d), SLOT_WIDTH ** -0.5),
        'conv_w': nrm((DEPTH, CONV_K, CONV_WIDTH), CONV_K ** -0.5),
        'conv_b': nrm((DEPTH, CONV_WIDTH), 0.02),
        'conv_ln_g': 1.0 + nrm((DEPTH, CONV_WIDTH), 0.02),
        'conv_ln_b': nrm((DEPTH, CONV_WIDTH), 0.02),
        'w_conv_out': nrm((DEPTH, CONV_WIDTH, d), CONV_WIDTH ** -0.5),
        'w_mix_out': nrm((DEPTH, d, d), d ** -0.5 * DN_BETA),
        'ln1_g': 1.0 + nrm((DEPTH, d), 0.02),
        'ln1_b': nrm((DEPTH, d), 0.02),
        'w_xq': nrm((DEPTH, d, d), d ** -0.5),
        'w_xkv': nrm((DEPTH, d, 2 * d), d ** -0.5) * xkv_scale,
        'w_xo': nrm((DEPTH, d, d), d ** -0.5 * DN_BETA),
        'ln2_g': 1.0 + nrm((DEPTH, d), 0.02),
        'ln2_b': nrm((DEPTH, d), 0.02),
        'w_up': nrm((DEPTH, d, 2 * D_FF), d ** -0.5 * DN_BETA),
        'ffn_conv_w': nrm((DEPTH, FFN_K, 2 * D_FF), FFN_K ** -0.5),
        'ffn_conv_b': nrm((DEPTH, 2 * D_FF), 0.02),
        'w_down': nrm((DEPTH, D_FF, d), D_FF ** -0.5 * DN_BETA),
        'ln3_g': 1.0 + nrm((DEPTH, d), 0.02),
        'ln3_b': nrm((DEPTH, d), 0.02),
    }


def reference(x_prompt, x_sample, cache_win1_kv, cache_win2_kv, cache_win3_kv, state_conv,
              state_ffn_conv, cache_mem_kv, mem_prompt, w_in, b_gate, w_attn_out, conv_w, conv_b,
              conv_ln_g, conv_ln_b, w_conv_out, w_mix_out, ln1_g, ln1_b, w_xq, w_xkv, w_xo,
              ln2_g, ln2_b, w_up, ffn_conv_w, ffn_conv_b, w_down, ln3_g, ln3_b):
    win_caches = (cache_win1_kv, cache_win2_kv, cache_win3_kv)
    bp, sp, _ = x_prompt.shape
    ts = x_sample.shape[1]
    pos_p = jnp.arange(sp)
    pos_s = PAST_LEN + jnp.arange(ts)
    xp, xs = x_prompt, x_sample
    win_p = [[] for _ in DIL_GROUPS]
    win_s = [[] for _ in DIL_GROUPS]
    conv_p, ffn_p, memkv_p, conv_s, ffn_s = [], [], [], [], []
    for l in range(DEPTH):
        lw = (w_in[l], b_gate[l], w_attn_out[l], conv_w[l], conv_b[l], conv_ln_g[l], conv_ln_b[l],
              w_conv_out[l], w_mix_out[l], ln1_g[l], ln1_b[l], w_xq[l], w_xo[l], ln2_g[l], ln2_b[l],
              w_up[l], ffn_conv_w[l], ffn_conv_b[l], w_down[l], ln3_g[l], ln3_b[l])
        mkv = (mem_prompt @ w_xkv[l]).reshape(bp, N_MEM, 2, X_HEADS, X_HEAD_DIM)
        zc = jnp.zeros((bp, CONV_K - 1, CONV_WIDTH), x_prompt.dtype)
        zf = jnp.zeros((bp, FFN_K - 1, 2 * D_FF), x_prompt.dtype)
        xp, kvp, cp, fp = decoder_layer(xp, pos_p, prompt_attention, zc, zf, mkv, *lw)
        for g, (win, _) in enumerate(DIL_GROUPS):
            win_p[g].append(group_heads(kvp, g)[:, sp - min(win, sp):])
        conv_p.append(cp)
        ffn_p.append(fp)
        memkv_p.append(mkv)
        attend_s = functools.partial(sample_attention, tuple(c[l] for c in win_caches))
        xs, kvs, cs, fs = decoder_layer(xs, pos_s, attend_s, state_conv[l], state_ffn_conv[l],
                                        cache_mem_kv[l], *lw)
        for g in range(N_GROUPS):
            win_s[g].append(group_heads(kvs, g))
        conv_s.append(cs)
        ffn_s.append(fs)
    y_prompt, y_sample = xp, xs
    win1_prompt, win2_prompt, win3_prompt = [jnp.stack(w) for w in win_p]
    win1_sample, win2_sample, win3_sample = [jnp.stack(w) for w in win_s]
    conv_prompt = jnp.stack(conv_p)
    ffn_prompt = jnp.stack(ffn_p)
    mem_kv_prompt = jnp.stack(memkv_p)
    conv_sample = jnp.stack(conv_s)
    ffn_sample = jnp.stack(ffn_s)
    return (y_prompt, y_sample, win1_prompt, win2_prompt, win3_prompt, conv_prompt, ffn_prompt,
            mem_kv_prompt, win1_sample, win2_sample, win3_sample, conv_sample, ffn_sample)
```

```python
import functools

import jax
import jax.numpy as jnp
from jax import lax
from jax.experimental import pallas as pl
from jax.experimental.pallas import tpu as pltpu

F32 = jnp.float32
BF16 = jnp.bfloat16

D_MODEL = 2048
PAST_LEN = 2048
HEAD_DIM = 128
HEADS_PER_GROUP = 4
DIL_GROUPS = ((128, 1), (512, 4), (2048, 16))
N_GROUPS = len(DIL_GROUPS)
ATTN_WIDTH = N_GROUPS * HEADS_PER_GROUP * HEAD_DIM
SLOT_WIDTH = HEADS_PER_GROUP * HEAD_DIM
ROT_DIM = HEAD_DIM // 4
ROT_HALF = ROT_DIM // 2
ROPE_THETA = 500000.0
CONV_WIDTH = 3 * D_MODEL // 4
CONV_K = 31
N_MEM = 256
X_HEADS = 4
X_HEAD_DIM = D_MODEL // X_HEADS
D_FF = 11 * D_MODEL // 4
FFN_K = 3
LN_EPS = 1e-5
DEPTH = 2
DN_ALPHA = (2 * DEPTH) ** 0.25

OFF_Q = 0
OFF_KV = ATTN_WIDTH
OFF_A = 3 * ATTN_WIDTH
OFF_BG = OFF_A + CONV_WIDTH
OFF_GATES = OFF_BG + CONV_WIDTH

LANES = 128
SUBLANES = 8
KEYS_PER_GROUP = 128
VMEM_CAP = 56 * 1024 * 1024
MASK_VALUE = -1e30


def _params(n_axes, vmem_bytes):
    return pltpu.CompilerParams(dimension_semantics=("arbitrary",) * n_axes,
                                vmem_limit_bytes=int(min(vmem_bytes, VMEM_CAP)))


def _layer_norm(x, g, b):
    mu = jnp.mean(x, axis=-1, keepdims=True)
    xc = x - mu
    var = jnp.mean(xc * xc, axis=-1, keepdims=True)
    return xc * lax.rsqrt(var + LN_EPS) * g + b


def _dot(a, b):
    return jnp.dot(a, b, preferred_element_type=F32)


def _dot_nt(a, b):
    return lax.dot_general(a, b, (((1,), (1,)), ((), ())), preferred_element_type=F32)


def _rope_proj_kernel(x_ref, w_ref, c_ref, s1_ref, s2_ref, *out_refs, n_rope, scale):
    n = pl.program_id(1)
    acc = _dot(x_ref[...], w_ref[...])
    tn = acc.shape[1]

    @pl.when(n < n_rope)
    def _():
        c, s1, s2 = c_ref[...], s1_ref[...], s2_ref[...]
        for h in range(tn // HEAD_DIM):
            sl = slice(h * HEAD_DIM, (h + 1) * HEAD_DIM)
            xh = acc[:, sl]
            r = (xh * c + pltpu.roll(xh, HEAD_DIM - ROT_HALF, 1) * s1
                 + pltpu.roll(xh, ROT_HALF, 1) * s2)
            if scale != 1.0:
                r = r * scale
            for o in out_refs:
                o[:, sl] = r.astype(o.dtype)

    @pl.when(n >= n_rope)
    def _():
        for o in out_refs:
            o[...] = acc.astype(o.dtype)


def _rope_proj(x, w, col_off, n_cols, tables, out_dtypes, n_rope_cols, scale, tm, name):
    m, k = x.shape
    tn = 512
    c_tab, s1_tab, s2_tab = tables
    tab_spec = pl.BlockSpec((tm, HEAD_DIM), lambda i, j: (i, 0))
    off = col_off // tn
    vmem = 2 * (tm * k * 2 + k * tn * 2 + 3 * tm * HEAD_DIM * 4
                + sum(tm * tn * jnp.dtype(d).itemsize for d in out_dtypes)) + tm * tn * 16
    return pl.pallas_call(
        functools.partial(_rope_proj_kernel, n_rope=n_rope_cols // tn, scale=scale),
        out_shape=[jax.ShapeDtypeStruct((m, n_cols), d) for d in out_dtypes],
        grid=(m // tm, n_cols // tn),
        in_specs=[pl.BlockSpec((tm, k), lambda i, j: (i, 0)),
                  pl.BlockSpec((k, tn), lambda i, j: (0, j + off)),
                  tab_spec, tab_spec, tab_spec],
        out_specs=[pl.BlockSpec((tm, tn), lambda i, j: (i, j)) for _ in out_dtypes],
        compiler_params=_params(2, vmem + (8 << 20)),
        name=name,
    )(x, w, c_tab, s1_tab, s2_tab)


def _glu_kernel(x_ref, wa_ref, wb_ref, o_ref):
    x = x_ref[...]
    o_ref[...] = _dot(x, wa_ref[...]) * jax.nn.sigmoid(_dot(x, wb_ref[...]))


def _glu_proj(x, w, tm, name):
    m, k = x.shape
    tn = 512
    oa, ob = OFF_A // tn, OFF_BG // tn
    vmem = 2 * (tm * k * 2 + 2 * k * tn * 2 + tm * tn * 4) + tm * tn * 16
    return pl.pallas_call(
        _glu_kernel,
        out_shape=jax.ShapeDtypeStruct((m, CONV_WIDTH), F32),
        grid=(m // tm, CONV_WIDTH // tn),
        in_specs=[pl.BlockSpec((tm, k), lambda i, j: (i, 0)),
                  pl.BlockSpec((k, tn), lambda i, j: (0, j + oa)),
                  pl.BlockSpec((k, tn), lambda i, j: (0, j + ob))],
        out_specs=pl.BlockSpec((tm, tn), lambda i, j: (i, j)),
        compiler_params=_params(2, vmem + (8 << 20)),
        name=name,
    )(x, w, w)


def _gate_kernel(x_ref, w_ref, b_ref, o_ref):
    o_ref[...] = jax.nn.sigmoid(_dot(x_ref[...], w_ref[...]) + b_ref[...]).astype(o_ref.dtype)


def _gate_proj(x, w, b_gate, tm, name):
    m, k = x.shape
    tn = 512
    og = OFF_GATES // tn
    vmem = 2 * (tm * k * 2 + k * tn * 2 + tm * tn * 2) + tm * tn * 16
    return pl.pallas_call(
        _gate_kernel,
        out_shape=jax.ShapeDtypeStruct((m, 2 * D_MODEL), BF16),
        grid=(m // tm, 2 * D_MODEL // tn),
        in_specs=[pl.BlockSpec((tm, k), lambda i, j: (i, 0)),
                  pl.BlockSpec((k, tn), lambda i, j: (0, j + og)),
                  pl.BlockSpec((1, tn), lambda i, j: (0, j))],
        out_specs=pl.BlockSpec((tm, tn), lambda i, j: (i, j)),
        compiler_params=_params(2, vmem + (8 << 20)),
        name=name,
    )(x, w, b_gate)


def _mm_kernel(x_ref, w_ref, *out_refs, scale):
    acc = _dot(x_ref[...], w_ref[...])
    if scale != 1.0:
        acc = acc * scale
    for o in out_refs:
        o[...] = acc.astype(o.dtype)


def _matmul(x, w, out_dtypes, tm, name, scale=1.0):
    m, k = x.shape
    n_cols = w.shape[1]
    tn = 512
    vmem = 2 * (tm * k * 2 + k * tn * 2
                + sum(tm * tn * jnp.dtype(d).itemsize for d in out_dtypes)) + tm * tn * 8
    return pl.pallas_call(
        functools.partial(_mm_kernel, scale=scale),
        out_shape=[jax.ShapeDtypeStruct((m, n_cols), d) for d in out_dtypes],
        grid=(m // tm, n_cols // tn),
        in_specs=[pl.BlockSpec((tm, k), lambda i, j: (i, 0)),
                  pl.BlockSpec((k, tn), lambda i, j: (0, j))],
        out_specs=[pl.BlockSpec((tm, tn), lambda i, j: (i, j)) for _ in out_dtypes],
        compiler_params=_params(2, vmem + (8 << 20)),
        name=name,
    )(x, w)


def _win_attn_kernel(q_ref, kc_ref, kp_ref, vc_ref, vp_ref, o_ref, l_ref):
    i = pl.program_id(1)
    tq = q_ref.shape[0]
    row = lax.broadcasted_iota(jnp.int32, (tq, tq), 0)
    col = lax.broadcasted_iota(jnp.int32, (tq, tq), 1)
    mask_c = col <= row
    mask_p = jnp.logical_and(col >= row, i > 0)
    for h in range(HEADS_PER_GROUP):
        sl = slice(h * HEAD_DIM, (h + 1) * HEAD_DIM)
        q = q_ref[:, sl]
        sc = jnp.where(mask_c, _dot_nt(q, kc_ref[:, sl]), MASK_VALUE)
        sp = jnp.where(mask_p, _dot_nt(q, kp_ref[:, sl]), MASK_VALUE)
        m = jnp.maximum(jnp.max(sc, axis=-1, keepdims=True), jnp.max(sp, axis=-1, keepdims=True))
        pc = jnp.exp(sc - m)
        pp = jnp.exp(sp - m)
        den = jnp.sum(pc, axis=-1, keepdims=True) + jnp.sum(pp, axis=-1, keepdims=True)
        o = _dot(pc.astype(BF16), vc_ref[:, sl]) + _dot(pp.astype(BF16), vp_ref[:, sl])
        o_ref[:, sl] = o * (1.0 / den)
        l_ref[:, sl] = jnp.broadcast_to(m + jnp.log(den), (tq, HEAD_DIM))


def _win_attn_prompt(q, kv, g, dil):
    t = q.shape[0]
    tc = t // dil
    tq = KEYS_PER_GROUP
    qv = q.reshape(tc, dil * ATTN_WIDTH)
    kvv = kv.reshape(tc, dil * 2 * ATTN_WIDTH)
    nq, nkv = ATTN_WIDTH // SLOT_WIDTH, 2 * ATTN_WIDTH // SLOT_WIDTH
    blk = (tq, SLOT_WIDTH)
    cur = lambda off: (lambda r, i: (i, r * nkv + off))
    prev = lambda off: (lambda r, i: (jnp.maximum(i - 1, 0), r * nkv + off))
    o, lse = pl.pallas_call(
        _win_attn_kernel,
        out_shape=[jax.ShapeDtypeStruct((tc, dil * SLOT_WIDTH), F32)] * 2,
        grid=(dil, tc // tq),
        in_specs=[pl.BlockSpec(blk, lambda r, i: (i, r * nq + g)),
                  pl.BlockSpec(blk, cur(g)), pl.BlockSpec(blk, prev(g)),
                  pl.BlockSpec(blk, cur(nq + g)), pl.BlockSpec(blk, prev(nq + g))],
        out_specs=[pl.BlockSpec(blk, lambda r, i: (i, r))] * 2,
        compiler_params=_params(2, 24 << 20),
        name=f"win_attn_g{g}",
    )(qv, kvv, kvv, kvv, kvv)
    return o.reshape(t, SLOT_WIDTH), lse.reshape(t, SLOT_WIDTH)


def _combine_kernel(o1, l1, o2, l2, o3, l3, out_ref):
    la, lb, lc = l1[...], l2[...], l3[...]
    m = jnp.maximum(jnp.maximum(la, lb), lc)
    ea, eb, ec = jnp.exp(la - m), jnp.exp(lb - m), jnp.exp(lc - m)
    num = ea * o1[...] + eb * o2[...] + ec * o3[...]
    out_ref[...] = (num * (1.0 / (ea + eb + ec))).astype(out_ref.dtype)


def _combine_groups(parts, tm):
    t = parts[0].shape[0]
    spec = pl.BlockSpec((tm, SLOT_WIDTH), lambda i: (i, 0))
    return pl.pallas_call(
        _combine_kernel,
        out_shape=jax.ShapeDtypeStruct((t, SLOT_WIDTH), BF16),
        grid=(t // tm,),
        in_specs=[spec] * 6,
        out_specs=spec,
        compiler_params=_params(1, 32 << 20),
        name="combine_groups",
    )(*parts)


def _win_decode_kernel(q_ref, kvn_ref, c1_ref, c2_ref, c3_ref, o_ref):
    caches = (c1_ref, c2_ref, c3_ref)
    bs = q_ref.shape[0]

    def body(b, carry):
        for h in range(HEADS_PER_GROUP):
            scores = []
            for g in range(N_GROUPS):
                lo = g * SLOT_WIDTH + h * HEAD_DIM
                qh = q_ref[b, :, lo:lo + HEAD_DIM]
                kh = caches[g][b, :, h * HEAD_DIM:(h + 1) * HEAD_DIM]
                s = jnp.sum(kh * qh, axis=-1, keepdims=True)
                s_new = jnp.sum(kvn_ref[b, :, lo:lo + HEAD_DIM] * qh, axis=-1, keepdims=True)
                scores.append((s, s_new))
            m = scores[0][1]
            for s, s_new in scores:
                m = jnp.maximum(m, jnp.maximum(jnp.max(s, axis=0, keepdims=True), s_new))
            num = jnp.zeros((1, HEAD_DIM), F32)
            den = jnp.zeros((1, 1), F32)
            for g, (s, s_new) in enumerate(scores):
                lo = g * SLOT_WIDTH + h * HEAD_DIM
                p = jnp.exp(s - m)
                p_new = jnp.exp(s_new - m)
                vh = caches[g][b, :, SLOT_WIDTH + h * HEAD_DIM:SLOT_WIDTH + (h + 1) * HEAD_DIM]
                v_new = kvn_ref[b, :, ATTN_WIDTH + lo:ATTN_WIDTH + lo + HEAD_DIM]
                num = num + jnp.sum(p * vh, axis=0, keepdims=True) + p_new * v_new
                den = den + jnp.sum(p, axis=0, keepdims=True) + p_new
            o_ref[b, :, h * HEAD_DIM:(h + 1) * HEAD_DIM] = num * (1.0 / den)
        return carry

    lax.fori_loop(0, bs, body, 0)


def _win_attn_sample(q, kv_new, caches, layer):
    b = q.shape[0]
    bs = 8
    views, specs = [], []
    for cache, (win, dil) in zip(caches, DIL_GROUPS):
        n_past = cache.shape[2]
        assert n_past == win and n_past % dil == 0
        views.append(cache.reshape(cache.shape[0] * b, n_past // dil, dil * 2 * SLOT_WIDTH))
        specs.append(pl.BlockSpec((bs, KEYS_PER_GROUP, 2 * SLOT_WIDTH),
                                  lambda i: (layer * (b // bs) + i, 0, 0)))
    vmem = 2 * (3 * bs * KEYS_PER_GROUP * 2 * SLOT_WIDTH * 4) + (8 << 20)
    out = pl.pallas_call(
        _win_decode_kernel,
        out_shape=jax.ShapeDtypeStruct((b, 1, SLOT_WIDTH), F32),
        grid=(b // bs,),
        in_specs=[pl.BlockSpec((bs, 1, ATTN_WIDTH), lambda i: (i, 0, 0)),
                  pl.BlockSpec((bs, 1, 2 * ATTN_WIDTH), lambda i: (i, 0, 0))] + specs,
        out_specs=pl.BlockSpec((bs, 1, SLOT_WIDTH), lambda i: (i, 0, 0)),
        compiler_params=_params(1, vmem),
        name="win_attn_sample",
    )(q.reshape(b, 1, ATTN_WIDTH), kv_new.reshape(b, 1, 2 * ATTN_WIDTH), *views)
    return out.reshape(b, SLOT_WIDTH)


CONV_HALO = 32
CONV_ROWS = 32
LN_ROWS = 16


def _conv_prompt_kernel(u_ref, halo_ref, w_ref, b_ref, g_ref, beta_ref, o_ref, ext_ref, sh_ref, c_ref):
    i = pl.program_id(0)
    tt = u_ref.shape[0]
    first_tap = CONV_HALO - (CONV_K - 1)
    halo = halo_ref[...]
    ext_ref[0:CONV_HALO, :] = jnp.where(i > 0, halo, jnp.zeros_like(halo))
    ext_ref[CONV_HALO:, :] = u_ref[...]
    n_sh = tt + CONV_HALO - SUBLANES
    for s in range(1, SUBLANES):
        sh_ref[s - 1] = ext_ref[pl.ds(s, n_sh), :]

    def tap_rows(ci, carry):
        r0 = pl.multiple_of(ci * CONV_ROWS, CONV_ROWS)
        for lb in range(CONV_WIDTH // LANES):
            ls = slice(lb * LANES, (lb + 1) * LANES)
            acc = jnp.broadcast_to(b_ref[:, ls], (CONV_ROWS, LANES))
            for j in range(CONV_K):
                a, s = divmod(first_tap + j, SUBLANES)
                if s == 0:
                    src = ext_ref[pl.ds(r0 + a * SUBLANES, CONV_ROWS), ls]
                else:
                    src = sh_ref[s - 1, pl.ds(r0 + a * SUBLANES, CONV_ROWS), ls]
                acc = acc + src * w_ref[j:j + 1, ls]
            c_ref[pl.ds(r0, CONV_ROWS), ls] = acc
        return carry

    lax.fori_loop(0, tt // CONV_ROWS, tap_rows, 0)

    def ln_rows(ci, carry):
        r0 = pl.multiple_of(ci * LN_ROWS, LN_ROWS)
        y = _layer_norm(c_ref[pl.ds(r0, LN_ROWS), :], g_ref[...], beta_ref[...])
        o_ref[pl.ds(r0, LN_ROWS), :] = (y * jax.nn.sigmoid(y)).astype(o_ref.dtype)
        return carry

    lax.fori_loop(0, tt // LN_ROWS, ln_rows, 0)


def _conv_prompt(u, conv_w, conv_b, ln_g, ln_b):
    t = u.shape[0]
    tt = 256
    row = lambda a: a.reshape(1, CONV_WIDTH)
    full = lambda shape: pl.BlockSpec(shape, lambda i: (0, 0))
    hb = tt // CONV_HALO
    vmem = (2 * (tt + CONV_HALO) * CONV_WIDTH * 4 + 2 * tt * CONV_WIDTH * 2
            + (8 * (tt + CONV_HALO) + tt) * CONV_WIDTH * 4 + (8 << 20))
    return pl.pallas_call(
        _conv_prompt_kernel,
        out_shape=jax.ShapeDtypeStruct((t, CONV_WIDTH), BF16),
        grid=(t // tt,),
        in_specs=[pl.BlockSpec((tt, CONV_WIDTH), lambda i: (i, 0)),
                  pl.BlockSpec((CONV_HALO, CONV_WIDTH), lambda i: (jnp.maximum(i * hb - 1, 0), 0)),
                  full((CONV_K, CONV_WIDTH)), full((1, CONV_WIDTH)),
                  full((1, CONV_WIDTH)), full((1, CONV_WIDTH))],
        out_specs=pl.BlockSpec((tt, CONV_WIDTH), lambda i: (i, 0)),
        scratch_shapes=[pltpu.VMEM((tt + CONV_HALO, CONV_WIDTH), F32),
                        pltpu.VMEM((SUBLANES - 1, tt + CONV_HALO - SUBLANES, CONV_WIDTH), F32),
                        pltpu.VMEM((tt, CONV_WIDTH), F32)],
        compiler_params=_params(1, vmem),
        name="conv_prompt",
    )(u, u, conv_w, row(conv_b), row(ln_g), row(ln_b))


def _conv_sample_kernel(st_ref, u_ref, w_ref, b_ref, g_ref, beta_ref, o_ref):
    acc = u_ref[...] * w_ref[CONV_K - 1:CONV_K, :] + b_ref[...]
    for j in range(CONV_K - 1):
        acc = acc + st_ref[:, j * CONV_WIDTH:(j + 1) * CONV_WIDTH] * w_ref[j:j + 1, :]
    y = _layer_norm(acc, g_ref[...], beta_ref[...])
    o_ref[...] = (y * jax.nn.sigmoid(y)).astype(o_ref.dtype)


def _conv_sample(state, u, conv_w, conv_b, ln_g, ln_b, layer):
    b = u.shape[0]
    bs = 16
    hist = (CONV_K - 1) * CONV_WIDTH
    st = state.reshape(state.shape[0], b, hist)
    row = lambda a: a.reshape(1, CONV_WIDTH)
    full = lambda shape: pl.BlockSpec(shape, lambda i: (0, 0))
    return pl.pallas_call(
        _conv_sample_kernel,
        out_shape=jax.ShapeDtypeStruct((b, CONV_WIDTH), BF16),
        grid=(b // bs,),
        in_specs=[pl.BlockSpec((None, bs, hist), lambda i: (layer, i, 0)),
                  pl.BlockSpec((bs, CONV_WIDTH), lambda i: (i, 0)),
                  full((CONV_K, CONV_WIDTH)), full((1, CONV_WIDTH)),
                  full((1, CONV_WIDTH)), full((1, CONV_WIDTH))],
        out_specs=pl.BlockSpec((bs, CONV_WIDTH), lambda i: (i, 0)),
        compiler_params=_params(1, 2 * bs * hist * 4 + (8 << 20)),
        name="conv_sample",
    )(st, u, conv_w, row(conv_b), row(ln_g), row(ln_b))


def _mix_kernel(a_ref, c_ref, wa_ref, wc_ref, ga_ref, gc_ref, o_ref):
    a = _dot(a_ref[...], wa_ref[...])
    c = _dot(c_ref[...], wc_ref[...])
    o_ref[...] = (ga_ref[...].astype(F32) * a + gc_ref[...].astype(F32) * c).astype(o_ref.dtype)


def _mix(attn, cact, gates, w_attn_out, w_conv_out, tm):
    m = attn.shape[0]
    tn = 512
    ng = D_MODEL // tn
    vmem = 2 * (tm * (SLOT_WIDTH + CONV_WIDTH) * 2 + (SLOT_WIDTH + CONV_WIDTH) * tn * 2
                + 3 * tm * tn * 2) + tm * tn * 16 + (8 << 20)
    return pl.pallas_call(
        _mix_kernel,
        out_shape=jax.ShapeDtypeStruct((m, D_MODEL), BF16),
        grid=(m // tm, D_MODEL // tn),
        in_specs=[pl.BlockSpec((tm, SLOT_WIDTH), lambda i, j: (i, 0)),
                  pl.BlockSpec((tm, CONV_WIDTH), lambda i, j: (i, 0)),
                  pl.BlockSpec((SLOT_WIDTH, tn), lambda i, j: (0, j)),
                  pl.BlockSpec((CONV_WIDTH, tn), lambda i, j: (0, j)),
                  pl.BlockSpec((tm, tn), lambda i, j: (i, j)),
                  pl.BlockSpec((tm, tn), lambda i, j: (i, j + ng))],
        out_specs=pl.BlockSpec((tm, tn), lambda i, j: (i, j)),
        compiler_params=_params(2, vmem),
        name="mix",
    )(attn, cact, w_attn_out, w_conv_out, gates, gates)


def _res_ln_kernel(h_ref, w_ref, x_ref, g_ref, b_ref, o_ref, ob_ref):
    y = DN_ALPHA * x_ref[...] + _dot(h_ref[...], w_ref[...])
    y = _layer_norm(y, g_ref[...], b_ref[...])
    o_ref[...] = y
    ob_ref[...] = y.astype(ob_ref.dtype)


def _res_ln(h, w, x, g, b, tm, name):
    m, k = h.shape
    full = lambda shape: pl.BlockSpec(shape, lambda i: (0, 0))
    vmem = 2 * (tm * k * 2 + tm * D_MODEL * (4 + 4 + 2)) + k * D_MODEL * 2 + tm * D_MODEL * 12 + (4 << 20)
    return pl.pallas_call(
        _res_ln_kernel,
        out_shape=[jax.ShapeDtypeStruct((m, D_MODEL), F32), jax.ShapeDtypeStruct((m, D_MODEL), BF16)],
        grid=(m // tm,),
        in_specs=[pl.BlockSpec((tm, k), lambda i: (i, 0)),
                  pl.BlockSpec((k, D_MODEL), lambda i: (0, 0), pipeline_mode=pl.Buffered(1)),
                  pl.BlockSpec((tm, D_MODEL), lambda i: (i, 0)),
                  full((1, D_MODEL)), full((1, D_MODEL))],
        out_specs=[pl.BlockSpec((tm, D_MODEL), lambda i: (i, 0))] * 2,
        compiler_params=_params(1, vmem),
        name=name,
    )(h, w, x, g.reshape(1, D_MODEL), b.reshape(1, D_MODEL))


def _xattn_prompt_kernel(q_ref, kv_ref, o_ref):
    for h in range(X_HEADS):
        sl = slice(h * X_HEAD_DIM, (h + 1) * X_HEAD_DIM)
        s = _dot_nt(q_ref[:, sl], kv_ref[:, sl])
        p = jnp.exp(s - jnp.max(s, axis=-1, keepdims=True))
        den = jnp.sum(p, axis=-1, keepdims=True)
        o = _dot(p.astype(BF16), kv_ref[:, D_MODEL + h * X_HEAD_DIM:D_MODEL + (h + 1) * X_HEAD_DIM])
        o_ref[:, sl] = (o * (1.0 / den)).astype(o_ref.dtype)


def _xattn_prompt(q, mem_kv, tm):
    t = q.shape[0]
    return pl.pallas_call(
        _xattn_prompt_kernel,
        out_shape=jax.ShapeDtypeStruct((t, D_MODEL), BF16),
        grid=(t // tm,),
        in_specs=[pl.BlockSpec((tm, D_MODEL), lambda i: (i, 0)),
                  pl.BlockSpec((N_MEM, 2 * D_MODEL), lambda i: (0, 0))],
        out_specs=pl.BlockSpec((tm, D_MODEL), lambda i: (i, 0)),
        compiler_params=_params(1, 32 << 20),
        name="xattn_prompt",
    )(q, mem_kv)


def _xattn_sample_kernel(q_ref, kv_ref, o_ref):
    bs = q_ref.shape[0]
    nl = X_HEAD_DIM // LANES
    for b in range(bs):
        for h in range(X_HEADS):
            lo = h * X_HEAD_DIM
            acc = kv_ref[b, :, lo:lo + LANES] * q_ref[b, :, lo:lo + LANES]
            for c in range(1, nl):
                cs = slice(lo + c * LANES, lo + (c + 1) * LANES)
                acc = acc + kv_ref[b, :, cs] * q_ref[b, :, cs]
            s = jnp.sum(acc, axis=-1, keepdims=True)
            p = jnp.exp(s - jnp.max(s, axis=0, keepdims=True))
            inv = 1.0 / jnp.sum(p, axis=0, keepdims=True)
            for c in range(nl):
                cs = slice(lo + c * LANES, lo + (c + 1) * LANES)
                vs = slice(D_MODEL + lo + c * LANES, D_MODEL + lo + (c + 1) * LANES)
                o_ref[b, :, cs] = jnp.sum(p * kv_ref[b, :, vs], axis=0, keepdims=True) * inv


def _xattn_sample(q, cache_mem_kv, layer):
    b = q.shape[0]
    bs = 4
    kv = cache_mem_kv.reshape(cache_mem_kv.shape[0] * b, N_MEM, 2 * D_MODEL)
    out = pl.pallas_call(
        _xattn_sample_kernel,
        out_shape=jax.ShapeDtypeStruct((b, 1, D_MODEL), F32),
        grid=(b // bs,),
        in_specs=[pl.BlockSpec((bs, 1, D_MODEL), lambda i: (i, 0, 0)),
                  pl.BlockSpec((bs, N_MEM, 2 * D_MODEL), lambda i: (layer * (b // bs) + i, 0, 0))],
        out_specs=pl.BlockSpec((bs, 1, D_MODEL), lambda i: (i, 0, 0)),
        compiler_params=_params(1, 2 * bs * N_MEM * 2 * D_MODEL * 4 + (8 << 20)),
        name="xattn_sample",
    )(q.reshape(b, 1, D_MODEL), kv)
    return out.reshape(b, D_MODEL)


def _ffn_prompt_kernel(x_ref, xh_ref, wg_ref, wv_ref, cwg_ref, cwv_ref, cbg_ref, cbv_ref,
                       f_ref, lg_ref, lv_ref, eg_ref, ev_ref):
    m = pl.program_id(1)
    tm = x_ref.shape[0]
    x, xh = x_ref[...], xh_ref[...]

    def branch(w_ref, cw_ref, cb_ref, ext_ref, last_ref):
        h = _dot(x, w_ref[...])
        hh = _dot(xh, w_ref[...])
        ext_ref[0:SUBLANES, :] = jnp.where(m > 0, hh, jnp.zeros_like(hh))
        ext_ref[SUBLANES:, :] = h
        out = cb_ref[...] + cw_ref[FFN_K - 1:FFN_K, :] * h
        for j in range(FFN_K - 1):
            out = out + cw_ref[j:j + 1, :] * ext_ref[pl.ds(SUBLANES - (FFN_K - 1) + j, tm), :]

        @pl.when(m == pl.num_programs(1) - 1)
        def _():
            last_ref[...] = ext_ref[pl.ds(tm, SUBLANES), :]
        return out

    gate = branch(wg_ref, cwg_ref, cbg_ref, eg_ref, lg_ref)
    val = branch(wv_ref, cwv_ref, cbv_ref, ev_ref, lv_ref)
    f_ref[...] = (gate * jax.nn.sigmoid(gate) * val).astype(f_ref.dtype)


def _ffn_prompt(x, w_up, conv_w, conv_b, tm):
    t, k = x.shape
    tn = 512
    nt = D_FF // tn
    hb = tm // SUBLANES
    cb = conv_b.reshape(1, 2 * D_FF)
    vmem = (2 * (tm * k * 2 + 2 * k * tn * 2 + tm * tn * 2) + 2 * (tm + SUBLANES) * tn * 4
            + tm * tn * 24 + (8 << 20))
    f, last_g, last_v = pl.pallas_call(
        _ffn_prompt_kernel,
        out_shape=[jax.ShapeDtypeStruct((t, D_FF), BF16),
                   jax.ShapeDtypeStruct((SUBLANES, D_FF), F32),
                   jax.ShapeDtypeStruct((SUBLANES, D_FF), F32)],
        grid=(nt, t // tm),
        in_specs=[pl.BlockSpec((tm, k), lambda n, m: (m, 0)),
                  pl.BlockSpec((SUBLANES, k), lambda n, m: (jnp.maximum(m * hb - 1, 0), 0)),
                  pl.BlockSpec((k, tn), lambda n, m: (0, n)),
                  pl.BlockSpec((k, tn), lambda n, m: (0, n + nt)),
                  pl.BlockSpec((FFN_K, tn), lambda n, m: (0, n)),
                  pl.BlockSpec((FFN_K, tn), lambda n, m: (0, n + nt)),
                  pl.BlockSpec((1, tn), lambda n, m: (0, n)),
                  pl.BlockSpec((1, tn), lambda n, m: (0, n + nt))],
        out_specs=[pl.BlockSpec((tm, tn), lambda n, m: (m, n)),
                   pl.BlockSpec((SUBLANES, tn), lambda n, m: (0, n)),
                   pl.BlockSpec((SUBLANES, tn), lambda n, m: (0, n))],
        scratch_shapes=[pltpu.VMEM((tm + SUBLANES, tn), F32)] * 2,
        compiler_params=_params(2, vmem),
        name="ffn_prompt",
    )(x, x, w_up, w_up, conv_w, conv_w, cb, cb)
    h_last = jnp.concatenate([last_g, last_v], axis=1)[SUBLANES - (FFN_K - 1):]
    return f, h_last


def _ffn_sample_kernel(x_ref, wg_ref, wv_ref, s0g_ref, s0v_ref, s1g_ref, s1v_ref,
                       cwg_ref, cwv_ref, cbg_ref, cbv_ref, f_ref, hg_ref, hv_ref):
    x = x_ref[...]

    def branch(w_ref, s0_ref, s1_ref, cw_ref, cb_ref, h_ref):
        h = _dot(x, w_ref[...])
        h_ref[...] = h
        return cb_ref[...] + cw_ref[0:1, :] * s0_ref[...] + cw_ref[1:2, :] * s1_ref[...] + cw_ref[2:3, :] * h

    gate = branch(wg_ref, s0g_ref, s1g_ref, cwg_ref, cbg_ref, hg_ref)
    val = branch(wv_ref, s0v_ref, s1v_ref, cwv_ref, cbv_ref, hv_ref)
    f_ref[...] = (gate * jax.nn.sigmoid(gate) * val).astype(f_ref.dtype)


def _ffn_sample(x, w_up, state, conv_w, conv_b, layer):
    b, k = x.shape
    tn = 512
    nt = D_FF // tn
    st = state.reshape(state.shape[0], b, (FFN_K - 1) * 2 * D_FF)
    cb = conv_b.reshape(1, 2 * D_FF)
    st_spec = lambda off: pl.BlockSpec((None, b, tn), lambda n: (layer, 0, n + off))
    col = lambda rows, off: pl.BlockSpec((rows, tn), lambda n: (0, n + off))
    f, hg, hv = pl.pallas_call(
        _ffn_sample_kernel,
        out_shape=[jax.ShapeDtypeStruct((b, D_FF), BF16),
                   jax.ShapeDtypeStruct((b, D_FF), F32), jax.ShapeDtypeStruct((b, D_FF), F32)],
        grid=(nt,),
        in_specs=[pl.BlockSpec((b, k), lambda n: (0, 0)),
                  col(k, 0), col(k, nt),
                  st_spec(0), st_spec(nt), st_spec(2 * nt), st_spec(3 * nt),
                  col(FFN_K, 0), col(FFN_K, nt), col(1, 0), col(1, nt)],
        out_specs=[pl.BlockSpec((b, tn), lambda n: (0, n))] * 3,
        compiler_params=_params(1, 32 << 20),
        name="ffn_sample",
    )(x, w_up, w_up, st, st, st, st, conv_w, conv_w, cb, cb)
    return f, jnp.concatenate([hg, hv], axis=1)


def _rope_tables(pos):
    t = pos.shape[0]
    inv = ROPE_THETA ** (-jnp.arange(0, ROT_DIM, 2, dtype=F32) / ROT_DIM)
    ang = pos.astype(F32)[:, None] * inv[None, :]
    cos, sin = jnp.cos(ang), jnp.sin(ang)
    zeros = lambda n: jnp.zeros((t, n), F32)
    c = jnp.concatenate([cos, cos, jnp.ones((t, HEAD_DIM - ROT_DIM), F32)], axis=1)
    s1 = jnp.concatenate([-sin, zeros(HEAD_DIM - ROT_HALF)], axis=1)
    s2 = jnp.concatenate([zeros(ROT_HALF), sin, zeros(HEAD_DIM - ROT_DIM)], axis=1)
    return c, s1, s2


def _window_rows(kv, g, rows):
    k = kv[kv.shape[0] - rows:, g * SLOT_WIDTH:(g + 1) * SLOT_WIDTH]
    v = kv[kv.shape[0] - rows:, ATTN_WIDTH + g * SLOT_WIDTH:ATTN_WIDTH + (g + 1) * SLOT_WIDTH]
    return jnp.stack([k, v], axis=1).reshape(rows, 2, HEADS_PER_GROUP, HEAD_DIM)


def _mixer(x, xb, wl, tables, tm, is_prompt, tag):
    q_dtype = BF16 if is_prompt else F32
    (q,) = _rope_proj(xb, wl["w_in"], OFF_Q, ATTN_WIDTH, tables, [q_dtype], ATTN_WIDTH,
                      HEAD_DIM ** -0.5, tm, f"q_proj_{tag}")
    kv_dtypes = [F32, BF16] if is_prompt else [F32]
    kv = _rope_proj(xb, wl["w_in"], OFF_KV, 2 * ATTN_WIDTH, tables, kv_dtypes, ATTN_WIDTH,
                    1.0, tm, f"kv_proj_{tag}")
    u = _glu_proj(xb, wl["w_in"], tm, f"glu_proj_{tag}")
    gates = _gate_proj(xb, wl["w_in"], wl["b_gate"], tm, f"gate_proj_{tag}")
    return q, kv, u, gates


def _back_half(x, attn, cact, gates, wl, tm, tm_ln):
    mix = _mix(attn, cact, gates, wl["w_attn_out"], wl["w_conv_out"], tm)
    return _res_ln(mix, wl["w_mix_out"], x, wl["ln1_g"], wl["ln1_b"], tm_ln, "mix_out_ln1")


def kernel(x_prompt, x_sample, cache_win1_kv, cache_win2_kv, cache_win3_kv, state_conv, state_ffn_conv,
           cache_mem_kv, mem_prompt, w_in, b_gate, w_attn_out, conv_w, conv_b, conv_ln_g, conv_ln_b,
           w_conv_out, w_mix_out, ln1_g, ln1_b, w_xq, w_xkv, w_xo, ln2_g, ln2_b, w_up, ffn_conv_w,
           ffn_conv_b, w_down, ln3_g, ln3_b):
    bp, sp, _ = x_prompt.shape
    nb, ts, _ = x_sample.shape
    assert bp == 1 and ts == 1
    depth = w_in.shape[0]
    assert depth == DEPTH
    win_caches = (cache_win1_kv, cache_win2_kv, cache_win3_kv)

    tables_p = _rope_tables(jnp.arange(sp))
    tables_s = _rope_tables(jnp.full((nb,), PAST_LEN, jnp.int32))
    mem_b = mem_prompt.reshape(N_MEM, D_MODEL).astype(BF16)

    bf = lambda a: a.astype(BF16)
    wb = dict(w_in=bf(w_in), w_attn_out=bf(w_attn_out), w_conv_out=bf(w_conv_out), w_mix_out=bf(w_mix_out),
              w_xq=bf(w_xq), w_xkv=bf(w_xkv), w_xo=bf(w_xo), w_up=bf(w_up), w_down=bf(w_down))
    small = dict(b_gate=b_gate, conv_w=conv_w, conv_b=conv_b, conv_ln_g=conv_ln_g, conv_ln_b=conv_ln_b,
                 ln1_g=ln1_g, ln1_b=ln1_b, ln2_g=ln2_g, ln2_b=ln2_b, ffn_conv_w=ffn_conv_w,
                 ffn_conv_b=ffn_conv_b, ln3_g=ln3_g, ln3_b=ln3_b)

    xp = x_prompt.reshape(sp, D_MODEL)
    xs = x_sample.reshape(nb, D_MODEL)
    xpb, xsb = bf(xp), bf(xs)
    tm_p, tm_s = 1024, nb

    win_p = [[] for _ in DIL_GROUPS]
    win_s = [[] for _ in DIL_GROUPS]
    conv_p, ffn_p, memkv_p, conv_s, ffn_s = [], [], [], [], []
    for l in range(depth):
        wl = {k: v[l] for k, v in wb.items()}
        wl.update({k: v[l] for k, v in small.items()})
        wl["b_gate"] = wl["b_gate"].reshape(1, 2 * D_MODEL)

        mkv, mkv_b = _matmul(mem_b, wl["w_xkv"], [F32, BF16], N_MEM, "mem_kv")
        q, (kv, kv_b), u, gates = _mixer(xp, xpb, wl, tables_p, tm_p, True, "p")
        parts = []
        for g, (_, dil) in enumerate(DIL_GROUPS):
            parts.extend(_win_attn_prompt(q, kv_b, g, dil))
        attn = _combine_groups(parts, 512)
        cact = _conv_prompt(u, wl["conv_w"], wl["conv_b"], wl["conv_ln_g"], wl["conv_ln_b"])
        xp, xpb = _back_half(xp, attn, cact, gates, wl, tm_p, 512)
        (xq,) = _matmul(xpb, wl["w_xq"], [BF16], tm_p, "xq_p", scale=X_HEAD_DIM ** -0.5)
        xo = _xattn_prompt(xq, mkv_b, 512)
        xp, xpb = _res_ln(xo, wl["w_xo"], xp, wl["ln2_g"], wl["ln2_b"], 512, "xo_ln2")
        f, h_last = _ffn_prompt(xpb, wl["w_up"], wl["ffn_conv_w"], wl["ffn_conv_b"], 512)
        xp, xpb = _res_ln(f, wl["w_down"], xp, wl["ln3_g"], wl["ln3_b"], 256, "down_ln3")
        for g, (win, _) in enumerate(DIL_GROUPS):
            win_p[g].append(_window_rows(kv, g, min(win, sp))[None])
        conv_p.append(u[None, sp - (CONV_K - 1):])
        ffn_p.append(h_last[None])
        memkv_p.append(mkv.reshape(1, N_MEM, 2, X_HEADS, X_HEAD_DIM))

        q, (kv,), u, gates = _mixer(xs, xsb, wl, tables_s, tm_s, False, "s")
        attn = bf(_win_attn_sample(q, kv, win_caches, l))
        cact = _conv_sample(state_conv, u, wl["conv_w"], wl["conv_b"], wl["conv_ln_g"], wl["conv_ln_b"], l)
        xs, xsb = _back_half(xs, attn, cact, gates, wl, tm_s, tm_s)
        (xq,) = _matmul(xsb, wl["w_xq"], [F32], tm_s, "xq_s", scale=X_HEAD_DIM ** -0.5)
        xo = bf(_xattn_sample(xq, cache_mem_kv, l))
        xs, xsb = _res_ln(xo, wl["w_xo"], xs, wl["ln2_g"], wl["ln2_b"], tm_s, "xo_ln2")
        f, h_new = _ffn_sample(xsb, wl["w_up"], state_ffn_conv, wl["ffn_conv_w"], wl["ffn_conv_b"], l)
        xs, xsb = _res_ln(f, wl["w_down"], xs, wl["ln3_g"], wl["ln3_b"], tm_s, "down_ln3")
        for g in range(N_GROUPS):
            win_s[g].append(_window_rows(kv, g, nb).reshape(nb, 1, 2, HEADS_PER_GROUP, HEAD_DIM))
        conv_s.append(jnp.concatenate([state_conv[l][:, 1:], u[:, None]], axis=1))
        ffn_s.append(jnp.concatenate([state_ffn_conv[l][:, 1:], h_new[:, None]], axis=1))

    stack = lambda xs_: jnp.stack(xs_)
    return (xp.reshape(bp, sp, D_MODEL), xs.reshape(nb, ts, D_MODEL),
            stack(win_p[0]), stack(win_p[1]), stack(win_p[2]), stack(conv_p), stack(ffn_p), stack(memkv_p),
            stack(win_s[0]), stack(win_s[1]), stack(win_s[2]), stack(conv_s), stack(ffn_s))
```

```python
import functools

import jax
import jax.numpy as jnp
from jax import lax
from jax.experimental import pallas as pl
from jax.experimental.pallas import tpu as pltpu

F32 = jnp.float32
BF16 = jnp.bfloat16

D_MODEL = 2048
PAST_LEN = 2048
HEAD_DIM = 128
HEADS_PER_GROUP = 4
DIL_GROUPS = ((128, 1), (512, 4), (2048, 16))
DILATIONS = tuple(d for _, d in DIL_GROUPS)
N_GROUPS = len(DIL_GROUPS)
ATTN_WIDTH = N_GROUPS * HEADS_PER_GROUP * HEAD_DIM
SLOT_WIDTH = HEADS_PER_GROUP * HEAD_DIM
ROT_DIM = HEAD_DIM // 4
ROT_HALF = ROT_DIM // 2
ROPE_THETA = 500000.0
CONV_WIDTH = 3 * D_MODEL // 4
CONV_K = 31
N_MEM = 256
X_HEADS = 4
X_HEAD_DIM = D_MODEL // X_HEADS
D_FF = 11 * D_MODEL // 4
FFN_K = 3
LN_EPS = 1e-5
DEPTH = 2
DN_ALPHA = (2 * DEPTH) ** 0.25

OFF_Q = 0
OFF_KV = ATTN_WIDTH
OFF_A = 3 * ATTN_WIDTH
OFF_BG = OFF_A + CONV_WIDTH
OFF_GATES = OFF_BG + CONV_WIDTH

LANES = 128
SUBLANES = 8
KEYS_PER_GROUP = 128
VMEM_CAP = 56 * 1024 * 1024
MASK_VALUE = -1e30
TN = 512


def _params(n_axes, vmem_bytes):
    return pltpu.CompilerParams(dimension_semantics=("arbitrary",) * n_axes,
                                vmem_limit_bytes=int(min(vmem_bytes, VMEM_CAP)))


def _layer_norm(x, g, b):
    mu = jnp.mean(x, axis=-1, keepdims=True)
    xc = x - mu
    var = jnp.mean(xc * xc, axis=-1, keepdims=True)
    return xc * lax.rsqrt(var + LN_EPS) * g + b


def _dot(a, b):
    return jnp.dot(a, b, preferred_element_type=F32)


def _dot_nt(a, b):
    return lax.dot_general(a, b, (((1,), (1,)), ((), ())), preferred_element_type=F32)


def _w_spec(k, layer, off=0):
    return pl.BlockSpec((None, k, TN), lambda i, j: (layer, 0, j + off))


def _rope_head(xh, c, s1, s2, scale):
    r = xh * c + pltpu.roll(xh, HEAD_DIM - ROT_HALF, 1) * s1 + pltpu.roll(xh, ROT_HALF, 1) * s2
    return r * scale if scale != 1.0 else r


def _rope_proj_kernel(x_ref, w_ref, c_ref, s1_ref, s2_ref, *out_refs, n_rope, scale):
    n = pl.program_id(1)
    acc = _dot(x_ref[...], w_ref[...])

    @pl.when(n < n_rope)
    def _():
        c, s1, s2 = c_ref[...], s1_ref[...], s2_ref[...]
        for h in range(TN // HEAD_DIM):
            sl = slice(h * HEAD_DIM, (h + 1) * HEAD_DIM)
            r = _rope_head(acc[:, sl], c, s1, s2, scale)
            for o in out_refs:
                o[:, sl] = r.astype(o.dtype)

    @pl.when(n >= n_rope)
    def _():
        for o in out_refs:
            o[...] = acc.astype(o.dtype)


def _rope_proj(x, w, layer, col_off, n_cols, tables, out_dtypes, n_rope_cols, scale, tm, name):
    m, k = x.shape
    tab_spec = pl.BlockSpec((tm, HEAD_DIM), lambda i, j: (i, 0))
    vmem = 2 * (tm * k * 2 + k * TN * 2 + 3 * tm * HEAD_DIM * 4
                + sum(tm * TN * jnp.dtype(d).itemsize for d in out_dtypes)) + tm * TN * 16
    return pl.pallas_call(
        functools.partial(_rope_proj_kernel, n_rope=n_rope_cols // TN, scale=scale),
        out_shape=[jax.ShapeDtypeStruct((m, n_cols), d) for d in out_dtypes],
        grid=(m // tm, n_cols // TN),
        in_specs=[pl.BlockSpec((tm, k), lambda i, j: (i, 0)), _w_spec(k, layer, col_off // TN),
                  tab_spec, tab_spec, tab_spec],
        out_specs=[pl.BlockSpec((tm, TN), lambda i, j: (i, j)) for _ in out_dtypes],
        compiler_params=_params(2, vmem + (8 << 20)),
        name=name,
    )(x, w, *tables)


def _perm_proj_kernel(x_ref, w_ref, c_ref, s1_ref, s2_ref, *refs, n_rope, scale, dils, n_plain):
    n = pl.program_id(1)
    perm_refs = refs[:len(dils)]
    plain_refs = refs[len(dils):len(dils) + n_plain]
    scr = refs[-1]
    acc = _dot(x_ref[...], w_ref[...])
    tm = acc.shape[0]
    heads = TN // HEAD_DIM

    @pl.when(n < n_rope)
    def _():
        c, s1, s2 = c_ref[...], s1_ref[...], s2_ref[...]
        for h in range(heads):
            scr[h] = _rope_head(acc[:, h * HEAD_DIM:(h + 1) * HEAD_DIM], c, s1, s2, scale)

    @pl.when(n >= n_rope)
    def _():
        for h in range(heads):
            scr[h] = acc[:, h * HEAD_DIM:(h + 1) * HEAD_DIM]

    for o in plain_refs:
        for h in range(heads):
            o[:, h * HEAD_DIM:(h + 1) * HEAD_DIM] = scr[h].astype(o.dtype)

    for j, (o, dil) in enumerate(zip(perm_refs, dils)):
        @pl.when(n == j)
        def _(o=o, dil=dil):
            for h in range(heads):
                sl = slice(h * HEAD_DIM, (h + 1) * HEAD_DIM)
                for r in range(dil):
                    rows = scr[h] if dil == 1 else scr[h, pl.ds(r, tm // dil, stride=dil), :]
                    o[r, :, sl] = rows.astype(o.dtype)


def _perm_proj(x, w, layer, col_off, dils, tables, n_rope_tiles, scale, tm, plain_dtypes, name):
    m, k = x.shape
    n_tiles = len(dils)
    tab_spec = pl.BlockSpec((tm, HEAD_DIM), lambda i, j: (i, 0))
    perm_shapes = [jax.ShapeDtypeStruct((d, m // d, TN), BF16) for d in dils]
    perm_specs = [pl.BlockSpec((d, tm // d, TN), lambda i, j: (0, i, 0)) for d in dils]
    plain_shapes = [jax.ShapeDtypeStruct((m, n_tiles * TN), d) for d in plain_dtypes]
    plain_specs = [pl.BlockSpec((tm, TN), lambda i, j: (i, j)) for _ in plain_dtypes]
    vmem = (2 * (tm * k * 2 + k * TN * 2 + 3 * tm * HEAD_DIM * 4 + n_tiles * tm * TN * 2
                 + sum(tm * TN * jnp.dtype(d).itemsize for d in plain_dtypes)) + tm * TN * 20)
    return pl.pallas_call(
        functools.partial(_perm_proj_kernel, n_rope=n_rope_tiles, scale=scale, dils=tuple(dils),
                          n_plain=len(plain_dtypes)),
        out_shape=perm_shapes + plain_shapes,
        grid=(m // tm, n_tiles),
        in_specs=[pl.BlockSpec((tm, k), lambda i, j: (i, 0)), _w_spec(k, layer, col_off // TN),
                  tab_spec, tab_spec, tab_spec],
        out_specs=perm_specs + plain_specs,
        scratch_shapes=[pltpu.VMEM((TN // HEAD_DIM, tm, HEAD_DIM), F32)],
        compiler_params=_params(2, vmem + (8 << 20)),
        name=name,
    )(x, w, *tables)


def _glu_kernel(x_ref, wa_ref, wb_ref, o_ref):
    x = x_ref[...]
    o_ref[...] = _dot(x, wa_ref[...]) * jax.nn.sigmoid(_dot(x, wb_ref[...]))


def _glu_proj(x, w, layer, tm, name):
    m, k = x.shape
    vmem = 2 * (tm * k * 2 + 2 * k * TN * 2 + tm * TN * 4) + tm * TN * 16
    return pl.pallas_call(
        _glu_kernel,
        out_shape=jax.ShapeDtypeStruct((m, CONV_WIDTH), F32),
        grid=(m // tm, CONV_WIDTH // TN),
        in_specs=[pl.BlockSpec((tm, k), lambda i, j: (i, 0)),
                  _w_spec(k, layer, OFF_A // TN), _w_spec(k, layer, OFF_BG // TN)],
        out_specs=pl.BlockSpec((tm, TN), lambda i, j: (i, j)),
        compiler_params=_params(2, vmem + (8 << 20)),
        name=name,
    )(x, w, w)


def _gate_kernel(x_ref, w_ref, b_ref, o_ref):
    o_ref[...] = jax.nn.sigmoid(_dot(x_ref[...], w_ref[...]) + b_ref[...]).astype(o_ref.dtype)


def _gate_proj(x, w, layer, b_gate, tm, name):
    m, k = x.shape
    vmem = 2 * (tm * k * 2 + k * TN * 2 + tm * TN * 2) + tm * TN * 16
    return pl.pallas_call(
        _gate_kernel,
        out_shape=jax.ShapeDtypeStruct((m, 2 * D_MODEL), BF16),
        grid=(m // tm, 2 * D_MODEL // TN),
        in_specs=[pl.BlockSpec((tm, k), lambda i, j: (i, 0)), _w_spec(k, layer, OFF_GATES // TN),
                  pl.BlockSpec((1, TN), lambda i, j: (0, j))],
        out_specs=pl.BlockSpec((tm, TN), lambda i, j: (i, j)),
        compiler_params=_params(2, vmem + (8 << 20)),
        name=name,
    )(x, w, b_gate)


def _mm_kernel(x_ref, w_ref, *out_refs, scale):
    acc = _dot(x_ref[...], w_ref[...])
    if scale != 1.0:
        acc = acc * scale
    for o in out_refs:
        o[...] = acc.astype(o.dtype)


def _matmul(x, w, layer, out_dtypes, tm, name, scale=1.0):
    m, k = x.shape
    n_cols = w.shape[2]
    vmem = 2 * (tm * k * 2 + k * TN * 2
                + sum(tm * TN * jnp.dtype(d).itemsize for d in out_dtypes)) + tm * TN * 8
    return pl.pallas_call(
        functools.partial(_mm_kernel, scale=scale),
        out_shape=[jax.ShapeDtypeStruct((m, n_cols), d) for d in out_dtypes],
        grid=(m // tm, n_cols // TN),
        in_specs=[pl.BlockSpec((tm, k), lambda i, j: (i, 0)), _w_spec(k, layer)],
        out_specs=[pl.BlockSpec((tm, TN), lambda i, j: (i, j)) for _ in out_dtypes],
        compiler_params=_params(2, vmem + (8 << 20)),
        name=name,
    )(x, w)


def _win_attn_kernel(q_ref, kc_ref, kp_ref, vc_ref, vp_ref, o_ref, l_ref):
    i = pl.program_id(1)
    tq = q_ref.shape[0]
    row = lax.broadcasted_iota(jnp.int32, (tq, tq), 0)
    col = lax.broadcasted_iota(jnp.int32, (tq, tq), 1)
    mask_c = col <= row
    mask_p = jnp.logical_and(col >= row, i > 0)
    for h in range(HEADS_PER_GROUP):
        sl = slice(h * HEAD_DIM, (h + 1) * HEAD_DIM)
        q = q_ref[:, sl]
        sc = jnp.where(mask_c, _dot_nt(q, kc_ref[:, sl]), MASK_VALUE)
        sp = jnp.where(mask_p, _dot_nt(q, kp_ref[:, sl]), MASK_VALUE)
        m = jnp.maximum(jnp.max(sc, axis=-1, keepdims=True), jnp.max(sp, axis=-1, keepdims=True))
        pc = jnp.exp(sc - m)
        pp = jnp.exp(sp - m)
        den = jnp.sum(pc, axis=-1, keepdims=True) + jnp.sum(pp, axis=-1, keepdims=True)
        o = _dot(pc.astype(BF16), vc_ref[:, sl]) + _dot(pp.astype(BF16), vp_ref[:, sl])
        o_ref[:, sl] = o * (1.0 / den)
        l_ref[:, sl] = jnp.broadcast_to(m + jnp.log(den), (tq, HEAD_DIM))


def _win_attn_prompt(q, k, v, g):
    dil, tc, _ = q.shape
    tq = KEYS_PER_GROUP
    blk = (None, tq, SLOT_WIDTH)
    cur = pl.BlockSpec(blk, lambda r, i: (r, i, 0))
    prev = pl.BlockSpec(blk, lambda r, i: (r, jnp.maximum(i - 1, 0), 0))
    return pl.pallas_call(
        _win_attn_kernel,
        out_shape=[jax.ShapeDtypeStruct((dil, tc, SLOT_WIDTH), F32)] * 2,
        grid=(dil, tc // tq),
        in_specs=[cur, cur, prev, cur, prev],
        out_specs=[cur, cur],
        compiler_params=_params(2, 24 << 20),
        name=f"win_attn_g{g}",
    )(q, k, k, v, v)


def _combine_kernel(o1, l1, o2, l2, o3, l3, out_ref, so2, sl2, so3, sl3):
    tm = out_ref.shape[0]
    lane_blocks = SLOT_WIDTH // LANES
    for src, dst, dil in ((o2, so2, DILATIONS[1]), (l2, sl2, DILATIONS[1]),
                          (o3, so3, DILATIONS[2]), (l3, sl3, DILATIONS[2])):
        for c in range(lane_blocks):
            for r in range(dil):
                dst[c, pl.ds(r, tm // dil, stride=dil), :] = src[r, :, c * LANES:(c + 1) * LANES]
    for c in range(lane_blocks):
        ls = slice(c * LANES, (c + 1) * LANES)
        la, lb, lc = l1[0, :, ls], sl2[c], sl3[c]
        m = jnp.maximum(jnp.maximum(la, lb), lc)
        ea, eb, ec = jnp.exp(la - m), jnp.exp(lb - m), jnp.exp(lc - m)
        num = ea * o1[0, :, ls] + eb * so2[c] + ec * so3[c]
        out_ref[:, ls] = (num * (1.0 / (ea + eb + ec))).astype(out_ref.dtype)


def _combine_groups(parts, tm):
    t = parts[0].shape[1]
    specs = []
    for dil in DILATIONS:
        specs += [pl.BlockSpec((dil, tm // dil, SLOT_WIDTH), lambda i: (0, i, 0))] * 2
    return pl.pallas_call(
        _combine_kernel,
        out_shape=jax.ShapeDtypeStruct((t, SLOT_WIDTH), BF16),
        grid=(t // tm,),
        in_specs=specs,
        out_specs=pl.BlockSpec((tm, SLOT_WIDTH), lambda i: (i, 0)),
        scratch_shapes=[pltpu.VMEM((SLOT_WIDTH // LANES, tm, LANES), F32)] * 4,
        compiler_params=_params(1, 32 << 20),
        name="combine_groups",
    )(*parts)


def _win_decode_kernel(q_ref, kvn_ref, c1_ref, c2_ref, c3_ref, o_ref):
    caches = (c1_ref, c2_ref, c3_ref)
    bs = q_ref.shape[0]

    def body(b, carry):
        scores = []
        for g in range(N_GROUPS):
            qg = q_ref[b, g]
            s = jnp.sum(caches[g][b, :, 0] * qg[None], axis=-1, keepdims=True)
            s_new = jnp.sum(kvn_ref[b, 0, g] * qg, axis=-1, keepdims=True)
            scores.append((s, s_new))
        m = scores[0][1]
        for s, s_new in scores:
            m = jnp.maximum(m, jnp.maximum(jnp.max(s, axis=0), s_new))
        num = jnp.zeros((HEADS_PER_GROUP, HEAD_DIM), F32)
        den = jnp.zeros((HEADS_PER_GROUP, 1), F32)
        for g, (s, s_new) in enumerate(scores):
            p = jnp.exp(s - m[None])
            p_new = jnp.exp(s_new - m)
            num = num + jnp.sum(p * caches[g][b, :, 1], axis=0) + p_new * kvn_ref[b, 1, g]
            den = den + jnp.sum(p, axis=0) + p_new
        o_ref[b] = num * (1.0 / den)
        return carry

    lax.fori_loop(0, bs, body, 0)


def _win_attn_sample(q, kv_new, caches, layer):
    b = q.shape[0]
    bs = 4
    views, specs = [], []
    for cache, (win, dil) in zip(caches, DIL_GROUPS):
        depth, _, n_past = cache.shape[:3]
        assert n_past == win and n_past % dil == 0
        views.append(cache.reshape(depth, b, n_past // dil, dil, 2, HEADS_PER_GROUP, HEAD_DIM))
        specs.append(pl.BlockSpec((None, bs, KEYS_PER_GROUP, None, 2, HEADS_PER_GROUP, HEAD_DIM),
                                  lambda i: (layer, i, 0, 0, 0, 0, 0)))
    q4 = q.reshape(b, N_GROUPS, HEADS_PER_GROUP, HEAD_DIM)
    kv5 = kv_new.reshape(b, 2, N_GROUPS, HEADS_PER_GROUP, HEAD_DIM)
    vmem = 2 * 2 * (3 * bs * KEYS_PER_GROUP * 2 * SLOT_WIDTH * 4) + (8 << 20)
    out = pl.pallas_call(
        _win_decode_kernel,
        out_shape=jax.ShapeDtypeStruct((b, HEADS_PER_GROUP, HEAD_DIM), F32),
        grid=(b // bs,),
        in_specs=[pl.BlockSpec((bs, N_GROUPS, HEADS_PER_GROUP, HEAD_DIM), lambda i: (i, 0, 0, 0)),
                  pl.BlockSpec((bs, 2, N_GROUPS, HEADS_PER_GROUP, HEAD_DIM), lambda i: (i, 0, 0, 0, 0))]
        + specs,
        out_specs=pl.BlockSpec((bs, HEADS_PER_GROUP, HEAD_DIM), lambda i: (i, 0, 0)),
        compiler_params=_params(1, vmem),
        name="win_attn_sample",
    )(q4, kv5, *views)
    return out.reshape(b, SLOT_WIDTH)


CONV_HALO = 32
CONV_ROWS = 32
LN_ROWS = 16


def _conv_prompt_kernel(u_ref, halo_ref, w_ref, b_ref, g_ref, beta_ref, o_ref, ext_ref, sh_ref, c_ref):
    i = pl.program_id(0)
    tt = u_ref.shape[0]
    first_tap = CONV_HALO - (CONV_K - 1)
    halo = halo_ref[...]
    ext_ref[0:CONV_HALO, :] = jnp.where(i > 0, halo, jnp.zeros_like(halo))
    ext_ref[CONV_HALO:, :] = u_ref[...]
    n_sh = tt + CONV_HALO - SUBLANES
    for s in range(1, SUBLANES):
        sh_ref[s - 1] = ext_ref[pl.ds(s, n_sh), :]

    def tap_rows(ci, carry):
        r0 = pl.multiple_of(ci * CONV_ROWS, CONV_ROWS)
        for lb in range(CONV_WIDTH // LANES):
            ls = slice(lb * LANES, (lb + 1) * LANES)
            acc = jnp.broadcast_to(b_ref[:, ls], (CONV_ROWS, LANES))
            for j in range(CONV_K):
                a, s = divmod(first_tap + j, SUBLANES)
                if s == 0:
                    src = ext_ref[pl.ds(r0 + a * SUBLANES, CONV_ROWS), ls]
                else:
                    src = sh_ref[s - 1, pl.ds(r0 + a * SUBLANES, CONV_ROWS), ls]
                acc = acc + src * w_ref[j:j + 1, ls]
            c_ref[pl.ds(r0, CONV_ROWS), ls] = acc
        return carry

    lax.fori_loop(0, tt // CONV_ROWS, tap_rows, 0)

    def ln_rows(ci, carry):
        r0 = pl.multiple_of(ci * LN_ROWS, LN_ROWS)
        y = _layer_norm(c_ref[pl.ds(r0, LN_ROWS), :], g_ref[...], beta_ref[...])
        o_ref[pl.ds(r0, LN_ROWS), :] = (y * jax.nn.sigmoid(y)).astype(o_ref.dtype)
        return carry

    lax.fori_loop(0, tt // LN_ROWS, ln_rows, 0)


def _conv_prompt(u, conv_w, conv_b, ln_g, ln_b):
    t = u.shape[0]
    tt = 256
    row = lambda a: a.reshape(1, CONV_WIDTH)
    full = lambda shape: pl.BlockSpec(shape, lambda i: (0, 0))
    hb = tt // CONV_HALO
    vmem = (2 * (tt + CONV_HALO) * CONV_WIDTH * 4 + 2 * tt * CONV_WIDTH * 2
            + (8 * (tt + CONV_HALO) + tt) * CONV_WIDTH * 4 + (8 << 20))
    return pl.pallas_call(
        _conv_prompt_kernel,
        out_shape=jax.ShapeDtypeStruct((t, CONV_WIDTH), BF16),
        grid=(t // tt,),
        in_specs=[pl.BlockSpec((tt, CONV_WIDTH), lambda i: (i, 0)),
                  pl.BlockSpec((CONV_HALO, CONV_WIDTH), lambda i: (jnp.maximum(i * hb - 1, 0), 0)),
                  full((CONV_K, CONV_WIDTH)), full((1, CONV_WIDTH)),
                  full((1, CONV_WIDTH)), full((1, CONV_WIDTH))],
        out_specs=pl.BlockSpec((tt, CONV_WIDTH), lambda i: (i, 0)),
        scratch_shapes=[pltpu.VMEM((tt + CONV_HALO, CONV_WIDTH), F32),
                        pltpu.VMEM((SUBLANES - 1, tt + CONV_HALO - SUBLANES, CONV_WIDTH), F32),
                        pltpu.VMEM((tt, CONV_WIDTH), F32)],
        compiler_params=_params(1, vmem),
        name="conv_prompt",
    )(u, u, conv_w, row(conv_b), row(ln_g), row(ln_b))


def _conv_sample_kernel(st_ref, u_ref, w_ref, b_ref, g_ref, beta_ref, o_ref, c_ref):
    bs = u_ref.shape[0]
    w_hist = w_ref[0:CONV_K - 1, :]
    for b in range(bs):
        c_ref[b:b + 1, :] = jnp.sum(st_ref[b] * w_hist, axis=0, keepdims=True)
    acc = c_ref[...] + u_ref[...] * w_ref[CONV_K - 1:CONV_K, :] + b_ref[...]
    y = _layer_norm(acc, g_ref[...], beta_ref[...])
    o_ref[...] = (y * jax.nn.sigmoid(y)).astype(o_ref.dtype)


def _conv_sample(state, u, conv_w, conv_b, ln_g, ln_b, layer):
    b = u.shape[0]
    bs = 16
    row = lambda a: a.reshape(1, CONV_WIDTH)
    full = lambda shape: pl.BlockSpec(shape, lambda i: (0, 0))
    return pl.pallas_call(
        _conv_sample_kernel,
        out_shape=jax.ShapeDtypeStruct((b, CONV_WIDTH), BF16),
        grid=(b // bs,),
        in_specs=[pl.BlockSpec((None, bs, CONV_K - 1, CONV_WIDTH), lambda i: (layer, i, 0, 0)),
                  pl.BlockSpec((bs, CONV_WIDTH), lambda i: (i, 0)),
                  full((CONV_K, CONV_WIDTH)), full((1, CONV_WIDTH)),
                  full((1, CONV_WIDTH)), full((1, CONV_WIDTH))],
        out_specs=pl.BlockSpec((bs, CONV_WIDTH), lambda i: (i, 0)),
        scratch_shapes=[pltpu.VMEM((bs, CONV_WIDTH), F32)],
        compiler_params=_params(1, 2 * bs * 32 * CONV_WIDTH * 4 + (8 << 20)),
        name="conv_sample",
    )(state, u, conv_w, row(conv_b), row(ln_g), row(ln_b))


def _mix_kernel(a_ref, c_ref, wa_ref, wc_ref, ga_ref, gc_ref, o_ref):
    a = _dot(a_ref[...], wa_ref[...])
    c = _dot(c_ref[...], wc_ref[...])
    o_ref[...] = (ga_ref[...].astype(F32) * a + gc_ref[...].astype(F32) * c).astype(o_ref.dtype)


def _mix(attn, cact, gates, w_attn_out, w_conv_out, layer, tm):
    m = attn.shape[0]
    ng = D_MODEL // TN
    vmem = 2 * (tm * (SLOT_WIDTH + CONV_WIDTH) * 2 + (SLOT_WIDTH + CONV_WIDTH) * TN * 2
                + 3 * tm * TN * 2) + tm * TN * 16 + (8 << 20)
    return pl.pallas_call(
        _mix_kernel,
        out_shape=jax.ShapeDtypeStruct((m, D_MODEL), BF16),
        grid=(m // tm, D_MODEL // TN),
        in_specs=[pl.BlockSpec((tm, SLOT_WIDTH), lambda i, j: (i, 0)),
                  pl.BlockSpec((tm, CONV_WIDTH), lambda i, j: (i, 0)),
                  _w_spec(SLOT_WIDTH, layer), _w_spec(CONV_WIDTH, layer),
                  pl.BlockSpec((tm, TN), lambda i, j: (i, j)),
                  pl.BlockSpec((tm, TN), lambda i, j: (i, j + ng))],
        out_specs=pl.BlockSpec((tm, TN), lambda i, j: (i, j)),
        compiler_params=_params(2, vmem),
        name="mix",
    )(attn, cact, w_attn_out, w_conv_out, gates, gates)


def _res_ln_kernel(h_ref, w_ref, x_ref, g_ref, b_ref, o_ref, ob_ref):
    y = DN_ALPHA * x_ref[...] + _dot(h_ref[...], w_ref[...])
    y = _layer_norm(y, g_ref[...], b_ref[...])
    o_ref[...] = y
    ob_ref[...] = y.astype(ob_ref.dtype)


def _res_ln(h, w, layer, x, g, b, tm, name):
    m, k = h.shape
    full = lambda shape: pl.BlockSpec(shape, lambda i: (0, 0))
    vmem = 2 * (tm * k * 2 + tm * D_MODEL * (4 + 4 + 2)) + k * D_MODEL * 2 + tm * D_MODEL * 12 + (4 << 20)
    return pl.pallas_call(
        _res_ln_kernel,
        out_shape=[jax.ShapeDtypeStruct((m, D_MODEL), F32), jax.ShapeDtypeStruct((m, D_MODEL), BF16)],
        grid=(m // tm,),
        in_specs=[pl.BlockSpec((tm, k), lambda i: (i, 0)),
                  pl.BlockSpec((None, k, D_MODEL), lambda i: (layer, 0, 0), pipeline_mode=pl.Buffered(1)),
                  pl.BlockSpec((tm, D_MODEL), lambda i: (i, 0)),
                  full((1, D_MODEL)), full((1, D_MODEL))],
        out_specs=[pl.BlockSpec((tm, D_MODEL), lambda i: (i, 0))] * 2,
        compiler_params=_params(1, vmem),
        name=name,
    )(h, w, x, g.reshape(1, D_MODEL), b.reshape(1, D_MODEL))


def _xattn_prompt_kernel(q_ref, kv_ref, o_ref):
    for h in range(X_HEADS):
        sl = slice(h * X_HEAD_DIM, (h + 1) * X_HEAD_DIM)
        s = _dot_nt(q_ref[:, sl], kv_ref[:, sl])
        p = jnp.exp(s - jnp.max(s, axis=-1, keepdims=True))
        den = jnp.sum(p, axis=-1, keepdims=True)
        o = _dot(p.astype(BF16), kv_ref[:, D_MODEL + h * X_HEAD_DIM:D_MODEL + (h + 1) * X_HEAD_DIM])
        o_ref[:, sl] = (o * (1.0 / den)).astype(o_ref.dtype)


def _xattn_prompt(q, mem_kv, tm):
    t = q.shape[0]
    return pl.pallas_call(
        _xattn_prompt_kernel,
        out_shape=jax.ShapeDtypeStruct((t, D_MODEL), BF16),
        grid=(t // tm,),
        in_specs=[pl.BlockSpec((tm, D_MODEL), lambda i: (i, 0)),
                  pl.BlockSpec((N_MEM, 2 * D_MODEL), lambda i: (0, 0))],
        out_specs=pl.BlockSpec((tm, D_MODEL), lambda i: (i, 0)),
        compiler_params=_params(1, 32 << 20),
        name="xattn_prompt",
    )(q, mem_kv)


XATTN_KEY_CHUNK = 32


def _xattn_sample_kernel(q_ref, kv_ref, o_ref):
    bs = q_ref.shape[0]
    nl = X_HEAD_DIM // LANES
    for b in range(bs):
        q = q_ref[b]

        def chunk(ci, carry):
            m, den, num = carry
            k0 = pl.multiple_of(ci * XATTN_KEY_CHUNK, XATTN_KEY_CHUNK)
            keys = pl.ds(k0, XATTN_KEY_CHUNK)
            acc = kv_ref[b, keys, 0, :, 0:LANES] * q[None, :, 0:LANES]
            for c in range(1, nl):
                ls = slice(c * LANES, (c + 1) * LANES)
                acc = acc + kv_ref[b, keys, 0, :, ls] * q[None, :, ls]
            s = jnp.sum(acc, axis=-1, keepdims=True)
            m_new = jnp.maximum(m, jnp.max(s, axis=0))
            alpha = jnp.exp(m - m_new)
            p = jnp.exp(s - m_new[None])
            den = den * alpha + jnp.sum(p, axis=0)
            num = num * alpha + jnp.sum(p * kv_ref[b, keys, 1], axis=0)
            return m_new, den, num

        init = (jnp.full((X_HEADS, 1), MASK_VALUE, F32), jnp.zeros((X_HEADS, 1), F32),
                jnp.zeros((X_HEADS, X_HEAD_DIM), F32))
        _, den, num = lax.fori_loop(0, N_MEM // XATTN_KEY_CHUNK, chunk, init)
        o_ref[b] = num * (1.0 / den)


def _xattn_sample(q, cache_mem_kv, layer):
    b = q.shape[0]
    bs = 2
    out = pl.pallas_call(
        _xattn_sample_kernel,
        out_shape=jax.ShapeDtypeStruct((b, X_HEADS, X_HEAD_DIM), F32),
        grid=(b // bs,),
        in_specs=[pl.BlockSpec((bs, X_HEADS, X_HEAD_DIM), lambda i: (i, 0, 0)),
                  pl.BlockSpec((None, bs, N_MEM, 2, X_HEADS, X_HEAD_DIM), lambda i: (layer, i, 0, 0, 0, 0))],
        out_specs=pl.BlockSpec((bs, X_HEADS, X_HEAD_DIM), lambda i: (i, 0, 0)),
        compiler_params=_params(1, 2 * 2 * bs * N_MEM * 2 * D_MODEL * 4 + (8 << 20)),
        name="xattn_sample",
    )(q.reshape(b, X_HEADS, X_HEAD_DIM), cache_mem_kv)
    return out.reshape(b, D_MODEL)


def _ffn_prompt_kernel(x_ref, xh_ref, wg_ref, wv_ref, cwg_ref, cwv_ref, cbg_ref, cbv_ref,
                       f_ref, lg_ref, lv_ref, eg_ref, ev_ref):
    m = pl.program_id(1)
    tm = x_ref.shape[0]
    x, xh = x_ref[...], xh_ref[...]

    def branch(w_ref, cw_ref, cb_ref, ext_ref, last_ref):
        h = _dot(x, w_ref[...])
        hh = _dot(xh, w_ref[...])
        ext_ref[0:SUBLANES, :] = jnp.where(m > 0, hh, jnp.zeros_like(hh))
        ext_ref[SUBLANES:, :] = h
        out = cb_ref[...] + cw_ref[FFN_K - 1:FFN_K, :] * h
        for j in range(FFN_K - 1):
            out = out + cw_ref[j:j + 1, :] * ext_ref[pl.ds(SUBLANES - (FFN_K - 1) + j, tm), :]

        @pl.when(m == pl.num_programs(1) - 1)
        def _():
            last_ref[...] = ext_ref[pl.ds(tm, SUBLANES), :]
        return out

    gate = branch(wg_ref, cwg_ref, cbg_ref, eg_ref, lg_ref)
    val = branch(wv_ref, cwv_ref, cbv_ref, ev_ref, lv_ref)
    f_ref[...] = (gate * jax.nn.sigmoid(gate) * val).astype(f_ref.dtype)


def _ffn_prompt(x, w_up, layer, conv_w, conv_b, tm):
    t, k = x.shape
    nt = D_FF // TN
    hb = tm // SUBLANES
    cb = conv_b.reshape(1, 2 * D_FF)
    vmem = (2 * (tm * k * 2 + 2 * k * TN * 2 + tm * TN * 2) + 2 * (tm + SUBLANES) * TN * 4
            + tm * TN * 24 + (8 << 20))
    f, last_g, last_v = pl.pallas_call(
        _ffn_prompt_kernel,
        out_shape=[jax.ShapeDtypeStruct((t, D_FF), BF16),
                   jax.ShapeDtypeStruct((SUBLANES, D_FF), F32),
                   jax.ShapeDtypeStruct((SUBLANES, D_FF), F32)],
        grid=(nt, t // tm),
        in_specs=[pl.BlockSpec((tm, k), lambda n, m: (m, 0)),
                  pl.BlockSpec((SUBLANES, k), lambda n, m: (jnp.maximum(m * hb - 1, 0), 0)),
                  pl.BlockSpec((None, k, TN), lambda n, m: (layer, 0, n)),
                  pl.BlockSpec((None, k, TN), lambda n, m: (layer, 0, n + nt)),
                  pl.BlockSpec((FFN_K, TN), lambda n, m: (0, n)),
                  pl.BlockSpec((FFN_K, TN), lambda n, m: (0, n + nt)),
                  pl.BlockSpec((1, TN), lambda n, m: (0, n)),
                  pl.BlockSpec((1, TN), lambda n, m: (0, n + nt))],
        out_specs=[pl.BlockSpec((tm, TN), lambda n, m: (m, n)),
                   pl.BlockSpec((SUBLANES, TN), lambda n, m: (0, n)),
                   pl.BlockSpec((SUBLANES, TN), lambda n, m: (0, n))],
        scratch_shapes=[pltpu.VMEM((tm + SUBLANES, TN), F32)] * 2,
        compiler_params=_params(2, vmem),
        name="ffn_prompt",
    )(x, x, w_up, w_up, conv_w, conv_w, cb, cb)
    h_last = jnp.concatenate([last_g, last_v], axis=1)[SUBLANES - (FFN_K - 1):]
    return f, h_last


def _ffn_sample_kernel(x_ref, wg_ref, wv_ref, s0g_ref, s0v_ref, s1g_ref, s1v_ref,
                       cwg_ref, cwv_ref, cbg_ref, cbv_ref, f_ref, hg_ref, hv_ref):
    x = x_ref[...]

    def branch(w_ref, s0_ref, s1_ref, cw_ref, cb_ref, h_ref):
        h = _dot(x, w_ref[...])
        h_ref[...] = h
        return cb_ref[...] + cw_ref[0:1, :] * s0_ref[...] + cw_ref[1:2, :] * s1_ref[...] + cw_ref[2:3, :] * h

    gate = branch(wg_ref, s0g_ref, s1g_ref, cwg_ref, cbg_ref, hg_ref)
    val = branch(wv_ref, s0v_ref, s1v_ref, cwv_ref, cbv_ref, hv_ref)
    f_ref[...] = (gate * jax.nn.sigmoid(gate) * val).astype(f_ref.dtype)


def _ffn_sample(x, w_up, state, conv_w, conv_b, layer):
    b, k = x.shape
    nt = D_FF // TN
    st = state.reshape(state.shape[0], b, (FFN_K - 1) * 2 * D_FF)
    cb = conv_b.reshape(1, 2 * D_FF)
    st_spec = lambda off: pl.BlockSpec((None, b, TN), lambda n: (layer, 0, n + off))
    w_spec = lambda off: pl.BlockSpec((None, k, TN), lambda n: (layer, 0, n + off))
    col = lambda rows, off: pl.BlockSpec((rows, TN), lambda n: (0, n + off))
    f, hg, hv = pl.pallas_call(
        _ffn_sample_kernel,
        out_shape=[jax.ShapeDtypeStruct((b, D_FF), BF16),
                   jax.ShapeDtypeStruct((b, D_FF), F32), jax.ShapeDtypeStruct((b, D_FF), F32)],
        grid=(nt,),
        in_specs=[pl.BlockSpec((b, k), lambda n: (0, 0)),
                  w_spec(0), w_spec(nt),
                  st_spec(0), st_spec(nt), st_spec(2 * nt), st_spec(3 * nt),
                  col(FFN_K, 0), col(FFN_K, nt), col(1, 0), col(1, nt)],
        out_specs=[pl.BlockSpec((b, TN), lambda n: (0, n))] * 3,
        compiler_params=_params(1, 32 << 20),
        name="ffn_sample",
    )(x, w_up, w_up, st, st, st, st, conv_w, conv_w, cb, cb)
    return f, jnp.concatenate([hg, hv], axis=1)


def _rope_tables(pos):
    t = pos.shape[0]
    inv = ROPE_THETA ** (-jnp.arange(0, ROT_DIM, 2, dtype=F32) / ROT_DIM)
    ang = pos.astype(F32)[:, None] * inv[None, :]
    cos, sin = jnp.cos(ang), jnp.sin(ang)
    zeros = lambda n: jnp.zeros((t, n), F32)
    c = jnp.concatenate([cos, cos, jnp.ones((t, HEAD_DIM - ROT_DIM), F32)], axis=1)
    s1 = jnp.concatenate([-sin, zeros(HEAD_DIM - ROT_HALF)], axis=1)
    s2 = jnp.concatenate([zeros(ROT_HALF), sin, zeros(HEAD_DIM - ROT_DIM)], axis=1)
    return c, s1, s2


def _window_rows(kv, g, rows):
    k = kv[kv.shape[0] - rows:, g * SLOT_WIDTH:(g + 1) * SLOT_WIDTH]
    v = kv[kv.shape[0] - rows:, ATTN_WIDTH + g * SLOT_WIDTH:ATTN_WIDTH + (g + 1) * SLOT_WIDTH]
    return jnp.stack([k, v], axis=1).reshape(rows, 2, HEADS_PER_GROUP, HEAD_DIM)


def kernel(x_prompt, x_sample, cache_win1_kv, cache_win2_kv, cache_win3_kv, state_conv, state_ffn_conv,
           cache_mem_kv, mem_prompt, w_in, b_gate, w_attn_out, conv_w, conv_b, conv_ln_g, conv_ln_b,
           w_conv_out, w_mix_out, ln1_g, ln1_b, w_xq, w_xkv, w_xo, ln2_g, ln2_b, w_up, ffn_conv_w,
           ffn_conv_b, w_down, ln3_g, ln3_b):
    bp, sp, _ = x_prompt.shape
    nb, ts, _ = x_sample.shape
    assert bp == 1 and ts == 1
    depth = w_in.shape[0]
    assert depth == DEPTH
    win_caches = (cache_win1_kv, cache_win2_kv, cache_win3_kv)

    tables_p = _rope_tables(jnp.arange(sp))
    tables_s = _rope_tables(jnp.full((nb,), PAST_LEN, jnp.int32))
    mem_b = mem_prompt.reshape(N_MEM, D_MODEL).astype(BF16)

    bf = lambda a: a.astype(BF16)
    w_in, w_attn_out, w_conv_out, w_mix_out = bf(w_in), bf(w_attn_out), bf(w_conv_out), bf(w_mix_out)
    w_xq, w_xkv, w_xo, w_up, w_down = bf(w_xq), bf(w_xkv), bf(w_xo), bf(w_up), bf(w_down)

    xp = x_prompt.reshape(sp, D_MODEL)
    xs = x_sample.reshape(nb, D_MODEL)
    xpb, xsb = bf(xp), bf(xs)
    tm_p, tm_s = 1024, nb
    q_scale = HEAD_DIM ** -0.5
    xq_scale = X_HEAD_DIM ** -0.5

    win_p = [[] for _ in DIL_GROUPS]
    win_s = [[] for _ in DIL_GROUPS]
    conv_p, ffn_p, memkv_p, conv_s, ffn_s = [], [], [], [], []
    for l in range(depth):
        bg = b_gate[l].reshape(1, 2 * D_MODEL)

        mkv, mkv_b = _matmul(mem_b, w_xkv, l, [F32, BF16], N_MEM, "mem_kv")
        qs = _perm_proj(xpb, w_in, l, OFF_Q, DILATIONS, tables_p, N_GROUPS, q_scale, tm_p, [], "q_proj_p")
        *kvs, kv = _perm_proj(xpb, w_in, l, OFF_KV, DILATIONS * 2, tables_p, N_GROUPS, 1.0, tm_p, [F32],
                              "kv_proj_p")
        u = _glu_proj(xpb, w_in, l, tm_p, "glu_proj_p")
        gates = _gate_proj(xpb, w_in, l, bg, tm_p, "gate_proj_p")
        parts = []
        for g in range(N_GROUPS):
            parts.extend(_win_attn_prompt(qs[g], kvs[g], kvs[N_GROUPS + g], g))
        attn = _combine_groups(parts, 512)
        cact = _conv_prompt(u, conv_w[l], conv_b[l], conv_ln_g[l], conv_ln_b[l])
        mix = _mix(attn, cact, gates, w_attn_out, w_conv_out, l, tm_p)
        xp, xpb = _res_ln(mix, w_mix_out, l, xp, ln1_g[l], ln1_b[l], 512, "mix_out_ln1")
        (xq,) = _matmul(xpb, w_xq, l, [BF16], tm_p, "xq_p", scale=xq_scale)
        xo = _xattn_prompt(xq, mkv_b, 512)
        xp, xpb = _res_ln(xo, w_xo, l, xp, ln2_g[l], ln2_b[l], 512, "xo_ln2")
        f, h_last = _ffn_prompt(xpb, w_up, l, ffn_conv_w[l], ffn_conv_b[l], 512)
        xp, xpb = _res_ln(f, w_down, l, xp, ln3_g[l], ln3_b[l], 256, "down_ln3")
        for g, (win, _) in enumerate(DIL_GROUPS):
            win_p[g].append(_window_rows(kv, g, min(win, sp))[None])
        conv_p.append(u[None, sp - (CONV_K - 1):])
        ffn_p.append(h_last[None])
        memkv_p.append(mkv.reshape(1, N_MEM, 2, X_HEADS, X_HEAD_DIM))

        (q,) = _rope_proj(xsb, w_in, l, OFF_Q, ATTN_WIDTH, tables_s, [F32], ATTN_WIDTH, q_scale, tm_s,
                          "q_proj_s")
        (kv,) = _rope_proj(xsb, w_in, l, OFF_KV, 2 * ATTN_WIDTH, tables_s, [F32], ATTN_WIDTH, 1.0, tm_s,
                           "kv_proj_s")
        u = _glu_proj(xsb, w_in, l, tm_s, "glu_proj_s")
        gates = _gate_proj(xsb, w_in, l, bg, tm_s, "gate_proj_s")
        attn = bf(_win_attn_sample(q, kv, win_caches, l))
        cact = _conv_sample(state_conv, u, conv_w[l], conv_b[l], conv_ln_g[l], conv_ln_b[l], l)
        mix = _mix(attn, cact, gates, w_attn_out, w_conv_out, l, tm_s)
        xs, xsb = _res_ln(mix, w_mix_out, l, xs, ln1_g[l], ln1_b[l], tm_s, "mix_out_ln1")
        (xq,) = _matmul(xsb, w_xq, l, [F32], tm_s, "xq_s", scale=xq_scale)
        xo = bf(_xattn_sample(xq, cache_mem_kv, l))
        xs, xsb = _res_ln(xo, w_xo, l, xs, ln2_g[l], ln2_b[l], tm_s, "xo_ln2")
        f, h_new = _ffn_sample(xsb, w_up, state_ffn_conv, ffn_conv_w[l], ffn_conv_b[l], l)
        xs, xsb = _res_ln(f, w_down, l, xs, ln3_g[l], ln3_b[l], tm_s, "down_ln3")
        for g in range(N_GROUPS):
            win_s[g].append(_window_rows(kv, g, nb).reshape(nb, 1, 2, HEADS_PER_GROUP, HEAD_DIM))
        conv_s.append(jnp.concatenate([state_conv[l][:, 1:], u[:, None]], axis=1))
        ffn_s.append(jnp.concatenate([state_ffn_conv[l][:, 1:], h_new[:, None]], axis=1))

    return (xp.reshape(bp, sp, D_MODEL), xs.reshape(nb, ts, D_MODEL),
            jnp.stack(win_p[0]), jnp.stack(win_p[1]), jnp.stack(win_p[2]), jnp.stack(conv_p),
            jnp.stack(ffn_p), jnp.stack(memkv_p),
            jnp.stack(win_s[0]), jnp.stack(win_s[1]), jnp.stack(win_s[2]), jnp.stack(conv_s),
            jnp.stack(ffn_s))
```

```python
import functools

import jax
import jax.numpy as jnp
from jax import lax
from jax.experimental import pallas as pl
from jax.experimental.pallas import tpu as pltpu

F32 = jnp.float32
BF16 = jnp.bfloat16

D_MODEL = 2048
PAST_LEN = 2048
HEAD_DIM = 128
HEADS_PER_GROUP = 4
DIL_GROUPS = ((128, 1), (512, 4), (2048, 16))
DILATIONS = tuple(d for _, d in DIL_GROUPS)
N_GROUPS = len(DIL_GROUPS)
ATTN_HEADS = N_GROUPS * HEADS_PER_GROUP
ATTN_WIDTH = ATTN_HEADS * HEAD_DIM
SLOT_WIDTH = HEADS_PER_GROUP * HEAD_DIM
ROT_DIM = HEAD_DIM // 4
ROT_HALF = ROT_DIM // 2
ROPE_THETA = 500000.0
CONV_WIDTH = 3 * D_MODEL // 4
CONV_K = 31
N_MEM = 256
X_HEADS = 4
X_HEAD_DIM = D_MODEL // X_HEADS
D_FF = 11 * D_MODEL // 4
FFN_K = 3
LN_EPS = 1e-5
DEPTH = 2
DN_ALPHA = (2 * DEPTH) ** 0.25

OFF_Q = 0
OFF_K = ATTN_WIDTH
OFF_V = 2 * ATTN_WIDTH
OFF_A = 3 * ATTN_WIDTH
OFF_BG = OFF_A + CONV_WIDTH
OFF_GATES = OFF_BG + CONV_WIDTH

LANES = 128
SUBLANES = 8
MXU_COLS = 256
KEYS_PER_GROUP = 128
VMEM_CAP = 56 * 1024 * 1024
MASK_VALUE = -1e30
TN = 512
CAST_ROWS = 256


def _params(n_axes, vmem_bytes):
    return pltpu.CompilerParams(dimension_semantics=("arbitrary",) * n_axes,
                                vmem_limit_bytes=int(min(vmem_bytes, VMEM_CAP)))


def _layer_norm(x, g, b):
    mu = jnp.mean(x, axis=-1, keepdims=True)
    xc = x - mu
    var = jnp.mean(xc * xc, axis=-1, keepdims=True)
    return xc * lax.rsqrt(var + LN_EPS) * g + b


def _dot(a, b):
    return jnp.dot(a, b, preferred_element_type=F32)


def _dot_nt(a, b):
    return lax.dot_general(a, b, (((1,), (1,)), ((), ())), preferred_element_type=F32)


def _w_spec(k, layer, off=0):
    return pl.BlockSpec((None, k, TN), lambda i, j: (layer, 0, j + off))


def _resident_w_spec(k, n, layer, col_block):
    return pl.BlockSpec((None, k, n), lambda i: (layer, 0, col_block), pipeline_mode=pl.Buffered(1))


def _cast_weight_once(w_ref, wb_ref):
    @pl.when(pl.program_id(0) == 0)
    def _():
        def rows(i, carry):
            r = pl.ds(pl.multiple_of(i * CAST_ROWS, CAST_ROWS), CAST_ROWS)
            wb_ref[r, :] = w_ref[r, :].astype(wb_ref.dtype)
            return carry
        lax.fori_loop(0, w_ref.shape[0] // CAST_ROWS, rows, 0)


def _rope_head(xh, c, s1, s2, scale):
    r = xh * c + pltpu.roll(xh, HEAD_DIM - ROT_HALF, 1) * s1 + pltpu.roll(xh, ROT_HALF, 1) * s2
    return r * scale if scale != 1.0 else r


def _qkv_kernel(x_ref, w_ref, c_ref, s1_ref, s2_ref, *refs, rope, scale, n_perm, n_plain):
    perm_refs = refs[:n_perm]
    plain_refs = refs[n_perm:n_perm + n_plain]
    wb_ref, scr = refs[n_perm + n_plain:]
    _cast_weight_once(w_ref, wb_ref)
    x = x_ref[...]
    tm = x.shape[0]
    heads_per_chunk = MXU_COLS // HEAD_DIM
    if rope:
        c, s1, s2 = c_ref[...], s1_ref[...], s2_ref[...]
    for chunk in range(ATTN_WIDTH // MXU_COLS):
        acc = _dot(x, wb_ref[:, chunk * MXU_COLS:(chunk + 1) * MXU_COLS])
        for hc in range(heads_per_chunk):
            head = chunk * heads_per_chunk + hc
            xh = acc[:, hc * HEAD_DIM:(hc + 1) * HEAD_DIM]
            if rope:
                xh = _rope_head(xh, c, s1, s2, scale)
            sl = slice(head * HEAD_DIM, (head + 1) * HEAD_DIM)
            for o in plain_refs:
                o[:, sl] = xh.astype(o.dtype)
            if n_perm:
                scr[head] = xh
    for g in range(n_perm):
        dil = DILATIONS[g]
        o = perm_refs[g]
        for h in range(HEADS_PER_GROUP):
            head = g * HEADS_PER_GROUP + h
            sl = slice(h * HEAD_DIM, (h + 1) * HEAD_DIM)
            for r in range(dil):
                rows = scr[head] if dil == 1 else scr[head, pl.ds(r, tm // dil, stride=dil), :]
                o[r, :, sl] = rows.astype(o.dtype)


def _qkv_proj(x, w_in, layer, col_off, tables, rope, scale, tm, perm, plain_dtypes, name):
    m, k = x.shape
    n_perm = N_GROUPS if perm else 0
    tab_spec = pl.BlockSpec((tm, HEAD_DIM), lambda i: (i, 0))
    perm_shapes = [jax.ShapeDtypeStruct((d, m // d, SLOT_WIDTH), BF16) for d in DILATIONS[:n_perm]]
    perm_specs = [pl.BlockSpec((d, tm // d, SLOT_WIDTH), lambda i: (0, i, 0)) for d in DILATIONS[:n_perm]]
    plain_shapes = [jax.ShapeDtypeStruct((m, ATTN_WIDTH), d) for d in plain_dtypes]
    plain_specs = [pl.BlockSpec((tm, ATTN_WIDTH), lambda i: (i, 0)) for _ in plain_dtypes]
    vmem = (2 * tm * k * 2 + k * ATTN_WIDTH * 6 + 2 * 3 * tm * HEAD_DIM * 4 + tm * ATTN_WIDTH * 4
            + 2 * n_perm * tm * SLOT_WIDTH * 2
            + 2 * sum(tm * ATTN_WIDTH * jnp.dtype(d).itemsize for d in plain_dtypes)
            + tm * MXU_COLS * 32 + (6 << 20))
    return pl.pallas_call(
        functools.partial(_qkv_kernel, rope=rope, scale=scale, n_perm=n_perm, n_plain=len(plain_dtypes)),
        out_shape=perm_shapes + plain_shapes,
        grid=(m // tm,),
        in_specs=[pl.BlockSpec((tm, k), lambda i: (i, 0)),
                  _resident_w_spec(k, ATTN_WIDTH, layer, col_off // ATTN_WIDTH),
                  tab_spec, tab_spec, tab_spec],
        out_specs=perm_specs + plain_specs,
        scratch_shapes=[pltpu.VMEM((k, ATTN_WIDTH), BF16), pltpu.VMEM((ATTN_HEADS, tm, HEAD_DIM), F32)],
        compiler_params=_params(1, vmem),
        name=name,
    )(x, w_in, *tables)


def _glu_kernel(x_ref, wa_ref, wb_ref, o_ref, wa_scr, wb_scr):
    _cast_weight_once(wa_ref, wa_scr)
    _cast_weight_once(wb_ref, wb_scr)
    x = x_ref[...]
    for chunk in range(CONV_WIDTH // MXU_COLS):
        cs = slice(chunk * MXU_COLS, (chunk + 1) * MXU_COLS)
        o_ref[:, cs] = _dot(x, wa_scr[:, cs]) * jax.nn.sigmoid(_dot(x, wb_scr[:, cs]))


def _glu_proj(x, w_in, layer, tm, name):
    m, k = x.shape
    vmem = 2 * tm * k * 2 + 2 * k * CONV_WIDTH * 6 + 2 * tm * CONV_WIDTH * 4 + tm * MXU_COLS * 32 + (4 << 20)
    return pl.pallas_call(
        _glu_kernel,
        out_shape=jax.ShapeDtypeStruct((m, CONV_WIDTH), F32),
        grid=(m // tm,),
        in_specs=[pl.BlockSpec((tm, k), lambda i: (i, 0)),
                  _resident_w_spec(k, CONV_WIDTH, layer, OFF_A // CONV_WIDTH),
                  _resident_w_spec(k, CONV_WIDTH, layer, OFF_BG // CONV_WIDTH)],
        out_specs=pl.BlockSpec((tm, CONV_WIDTH), lambda i: (i, 0)),
        scratch_shapes=[pltpu.VMEM((k, CONV_WIDTH), BF16)] * 2,
        compiler_params=_params(1, vmem),
        name=name,
    )(x, w_in, w_in)


def _gate_kernel(x_ref, w_ref, b_ref, o_ref):
    x = x_ref[...]
    for chunk in range(TN // MXU_COLS):
        cs = slice(chunk * MXU_COLS, (chunk + 1) * MXU_COLS)
        o_ref[:, cs] = jax.nn.sigmoid(_dot(x, w_ref[:, cs]) + b_ref[:, cs]).astype(o_ref.dtype)


def _gate_proj(x, w_gates, layer, b_gate, tm, name):
    m, k = x.shape
    vmem = 2 * (tm * k * 2 + k * TN * 2 + tm * TN * 2) + tm * TN * 16
    return pl.pallas_call(
        _gate_kernel,
        out_shape=jax.ShapeDtypeStruct((m, 2 * D_MODEL), BF16),
        grid=(m // tm, 2 * D_MODEL // TN),
        in_specs=[pl.BlockSpec((tm, k), lambda i, j: (i, 0)), _w_spec(k, layer),
                  pl.BlockSpec((1, TN), lambda i, j: (0, j))],
        out_specs=pl.BlockSpec((tm, TN), lambda i, j: (i, j)),
        compiler_params=_params(2, vmem + (8 << 20)),
        name=name,
    )(x, w_gates, b_gate)


def _mm_kernel(x_ref, w_ref, *out_refs, scale):
    acc = _dot(x_ref[...], w_ref[...])
    if scale != 1.0:
        acc = acc * scale
    for o in out_refs:
        o[...] = acc.astype(o.dtype)


def _matmul(x, w, layer, out_dtypes, tm, name, scale=1.0):
    m, k = x.shape
    n_cols = w.shape[2]
    vmem = 2 * (tm * k * 2 + k * TN * 2
                + sum(tm * TN * jnp.dtype(d).itemsize for d in out_dtypes)) + tm * TN * 8
    return pl.pallas_call(
        functools.partial(_mm_kernel, scale=scale),
        out_shape=[jax.ShapeDtypeStruct((m, n_cols), d) for d in out_dtypes],
        grid=(m // tm, n_cols // TN),
        in_specs=[pl.BlockSpec((tm, k), lambda i, j: (i, 0)), _w_spec(k, layer)],
        out_specs=[pl.BlockSpec((tm, TN), lambda i, j: (i, j)) for _ in out_dtypes],
        compiler_params=_params(2, vmem + (8 << 20)),
        name=name,
    )(x, w)


def _win_attn_kernel(q_ref, kc_ref, kp_ref, vc_ref, vp_ref, o_ref, l_ref):
    i = pl.program_id(1)
    tq = q_ref.shape[0]
    row = lax.broadcasted_iota(jnp.int32, (tq, tq), 0)
    col = lax.broadcasted_iota(jnp.int32, (tq, tq), 1)
    mask_c = col <= row
    mask_p = jnp.logical_and(col >= row, i > 0)
    for h in range(HEADS_PER_GROUP):
        sl = slice(h * HEAD_DIM, (h + 1) * HEAD_DIM)
        q = q_ref[:, sl]
        sc = jnp.where(mask_c, _dot_nt(q, kc_ref[:, sl]), MASK_VALUE)
        sp = jnp.where(mask_p, _dot_nt(q, kp_ref[:, sl]), MASK_VALUE)
        m = jnp.maximum(jnp.max(sc, axis=-1, keepdims=True), jnp.max(sp, axis=-1, keepdims=True))
        pc = jnp.exp(sc - m)
        pp = jnp.exp(sp - m)
        den = jnp.sum(pc, axis=-1, keepdims=True) + jnp.sum(pp, axis=-1, keepdims=True)
        o = _dot(pc.astype(BF16), vc_ref[:, sl]) + _dot(pp.astype(BF16), vp_ref[:, sl])
        o_ref[:, sl] = o * (1.0 / den)
        l_ref[:, sl] = jnp.broadcast_to(m + jnp.log(den), (tq, HEAD_DIM))


def _win_attn_prompt(q, k, v, g):
    dil, tc, _ = q.shape
    tq = KEYS_PER_GROUP
    blk = (None, tq, SLOT_WIDTH)
    cur = pl.BlockSpec(blk, lambda r, i: (r, i, 0))
    prev = pl.BlockSpec(blk, lambda r, i: (r, jnp.maximum(i - 1, 0), 0))
    return pl.pallas_call(
        _win_attn_kernel,
        out_shape=[jax.ShapeDtypeStruct((dil, tc, SLOT_WIDTH), F32)] * 2,
        grid=(dil, tc // tq),
        in_specs=[cur, cur, prev, cur, prev],
        out_specs=[cur, cur],
        compiler_params=_params(2, 24 << 20),
        name=f"win_attn_g{g}",
    )(q, k, k, v, v)


def _combine_kernel(o1, l1, o2, l2, o3, l3, out_ref, so2, sl2, so3, sl3):
    tm = out_ref.shape[0]
    lane_blocks = SLOT_WIDTH // LANES
    for src, dst, dil in ((o2, so2, DILATIONS[1]), (l2, sl2, DILATIONS[1]),
                          (o3, so3, DILATIONS[2]), (l3, sl3, DILATIONS[2])):
        for c in range(lane_blocks):
            for r in range(dil):
                dst[c, pl.ds(r, tm // dil, stride=dil), :] = src[r, :, c * LANES:(c + 1) * LANES]
    for c in range(lane_blocks):
        ls = slice(c * LANES, (c + 1) * LANES)
        la, lb, lc = l1[0, :, ls], sl2[c], sl3[c]
        m = jnp.maximum(jnp.maximum(la, lb), lc)
        ea, eb, ec = jnp.exp(la - m), jnp.exp(lb - m), jnp.exp(lc - m)
        num = ea * o1[0, :, ls] + eb * so2[c] + ec * so3[c]
        out_ref[:, ls] = (num * (1.0 / (ea + eb + ec))).astype(out_ref.dtype)


def _combine_groups(parts, tm):
    t = parts[0].shape[1]
    specs = []
    for dil in DILATIONS:
        specs += [pl.BlockSpec((dil, tm // dil, SLOT_WIDTH), lambda i: (0, i, 0))] * 2
    return pl.pallas_call(
        _combine_kernel,
        out_shape=jax.ShapeDtypeStruct((t, SLOT_WIDTH), BF16),
        grid=(t // tm,),
        in_specs=specs,
        out_specs=pl.BlockSpec((tm, SLOT_WIDTH), lambda i: (i, 0)),
        scratch_shapes=[pltpu.VMEM((SLOT_WIDTH // LANES, tm, LANES), F32)] * 4,
        compiler_params=_params(1, 32 << 20),
        name="combine_groups",
    )(*parts)


KV_ROWS = 2 * HEADS_PER_GROUP


def _win_decode_kernel(q_ref, kvn_ref, c1_ref, c2_ref, c3_ref, o_ref):
    caches = (c1_ref, c2_ref, c3_ref)
    bs = q_ref.shape[0]

    def body(b, carry):
        q8 = q_ref[b]
        scores = [jnp.sum(kvn_ref[b] * q8, axis=-1, keepdims=True)]
        for g in range(N_GROUPS):
            scores.append(jnp.sum(caches[g][b] * q8[g][None], axis=-1, keepdims=True))
        m = jnp.max(scores[0], axis=0)
        for s in scores[1:]:
            m = jnp.maximum(m, jnp.max(s, axis=0))
        num = jnp.zeros((KV_ROWS, HEAD_DIM), F32)
        den = jnp.zeros((KV_ROWS, 1), F32)
        for s, rows in zip(scores, [kvn_ref[b]] + [c[b] for c in caches]):
            p = jnp.exp(s - m[None])
            den = den + jnp.sum(p, axis=0)
            p_val = pltpu.roll(jnp.broadcast_to(p, rows.shape), HEADS_PER_GROUP, 1)
            num = num + jnp.sum(p_val * rows, axis=0)
        o_ref[b] = num * (1.0 / pltpu.roll(den, HEADS_PER_GROUP, 0))
        return carry

    lax.fori_loop(0, bs, body, 0, unroll=2)


def _win_attn_sample(q, k_new, v_new, caches, layer):
    b = q.shape[0]
    bs = 4
    views, specs = [], []
    for cache, (win, dil) in zip(caches, DIL_GROUPS):
        depth, _, n_past = cache.shape[:3]
        assert n_past == win and n_past % dil == 0
        views.append(cache.reshape(depth, b, n_past // dil, dil, KV_ROWS, HEAD_DIM))
        specs.append(pl.BlockSpec((None, bs, KEYS_PER_GROUP, None, KV_ROWS, HEAD_DIM),
                                  lambda i: (layer, i, 0, 0, 0, 0)))
    heads = lambda a: a.reshape(b, N_GROUPS, HEADS_PER_GROUP, HEAD_DIM)
    q8 = jnp.concatenate([heads(q), heads(q)], axis=2)
    kv8 = jnp.concatenate([heads(k_new), heads(v_new)], axis=2)
    row_spec = pl.BlockSpec((bs, N_GROUPS, KV_ROWS, HEAD_DIM), lambda i: (i, 0, 0, 0))
    vmem = 2 * (3 * bs * KEYS_PER_GROUP * KV_ROWS * HEAD_DIM * 4) + (16 << 20)
    out = pl.pallas_call(
        _win_decode_kernel,
        out_shape=jax.ShapeDtypeStruct((b, KV_ROWS, HEAD_DIM), F32),
        grid=(b // bs,),
        in_specs=[row_spec, row_spec] + specs,
        out_specs=pl.BlockSpec((bs, KV_ROWS, HEAD_DIM), lambda i: (i, 0, 0)),
        compiler_params=_params(1, vmem),
        name="win_attn_sample",
    )(q8, kv8, *views)
    return out[:, HEADS_PER_GROUP:].reshape(b, SLOT_WIDTH)


CONV_HALO = 32
CONV_ROWS = 64
LN_ROWS = 16


def _conv_prompt_kernel(u_ref, halo_ref, w_ref, b_ref, g_ref, beta_ref, o_ref, ext_ref, sh_ref, c_ref):
    i = pl.program_id(0)
    tt = u_ref.shape[0]
    first_tap = CONV_HALO - (CONV_K - 1)
    halo = halo_ref[...]
    ext_ref[0:CONV_HALO, :] = jnp.where(i > 0, halo, jnp.zeros_like(halo))
    ext_ref[CONV_HALO:, :] = u_ref[...]
    n_sh = tt + CONV_HALO - SUBLANES
    for s in range(1, SUBLANES):
        sh_ref[s - 1] = ext_ref[pl.ds(s, n_sh), :]

    n_tiles = CONV_ROWS // SUBLANES

    def tap_rows(ci, carry):
        r0 = pl.multiple_of(ci * CONV_ROWS, CONV_ROWS)
        for lb in range(CONV_WIDTH // LANES):
            ls = slice(lb * LANES, (lb + 1) * LANES)
            bias = jnp.broadcast_to(b_ref[:, ls], (SUBLANES, LANES))
            acc = [bias] * n_tiles
            for s in range(SUBLANES):
                taps = [(a, a * SUBLANES + s - first_tap) for a in range((CONV_HALO + SUBLANES) // SUBLANES)
                        if 0 <= a * SUBLANES + s - first_tap < CONV_K]
                a_lo, a_hi = taps[0][0], taps[-1][0]
                rows = pl.ds(r0 + a_lo * SUBLANES, CONV_ROWS + (a_hi - a_lo) * SUBLANES)
                strip = ext_ref[rows, ls] if s == 0 else sh_ref[s - 1, rows, ls]
                for a, j in taps:
                    wj = w_ref[j:j + 1, ls]
                    for k in range(n_tiles):
                        t0 = (k + a - a_lo) * SUBLANES
                        acc[k] = acc[k] + strip[t0:t0 + SUBLANES] * wj
            c_ref[pl.ds(r0, CONV_ROWS), ls] = jnp.concatenate(acc, axis=0)
        return carry

    lax.fori_loop(0, tt // CONV_ROWS, tap_rows, 0)

    def ln_rows(ci, carry):
        r0 = pl.multiple_of(ci * LN_ROWS, LN_ROWS)
        y = _layer_norm(c_ref[pl.ds(r0, LN_ROWS), :], g_ref[...], beta_ref[...])
        o_ref[pl.ds(r0, LN_ROWS), :] = (y * jax.nn.sigmoid(y)).astype(o_ref.dtype)
        return carry

    lax.fori_loop(0, tt // LN_ROWS, ln_rows, 0, unroll=4)


def _conv_prompt(u, conv_w, conv_b, ln_g, ln_b):
    t = u.shape[0]
    tt = 256
    row = lambda a: a.reshape(1, CONV_WIDTH)
    full = lambda shape: pl.BlockSpec(shape, lambda i: (0, 0))
    hb = tt // CONV_HALO
    vmem = (2 * (tt + CONV_HALO) * CONV_WIDTH * 4 + 2 * tt * CONV_WIDTH * 2
            + (8 * (tt + CONV_HALO) + tt) * CONV_WIDTH * 4 + (8 << 20))
    return pl.pallas_call(
        _conv_prompt_kernel,
        out_shape=jax.ShapeDtypeStruct((t, CONV_WIDTH), BF16),
        grid=(t // tt,),
        in_specs=[pl.BlockSpec((tt, CONV_WIDTH), lambda i: (i, 0)),
                  pl.BlockSpec((CONV_HALO, CONV_WIDTH), lambda i: (jnp.maximum(i * hb - 1, 0), 0)),
                  full((CONV_K, CONV_WIDTH)), full((1, CONV_WIDTH)),
                  full((1, CONV_WIDTH)), full((1, CONV_WIDTH))],
        out_specs=pl.BlockSpec((tt, CONV_WIDTH), lambda i: (i, 0)),
        scratch_shapes=[pltpu.VMEM((tt + CONV_HALO, CONV_WIDTH), F32),
                        pltpu.VMEM((SUBLANES - 1, tt + CONV_HALO - SUBLANES, CONV_WIDTH), F32),
                        pltpu.VMEM((tt, CONV_WIDTH), F32)],
        compiler_params=_params(1, vmem),
        name="conv_prompt",
    )(u, u, conv_w, row(conv_b), row(ln_g), row(ln_b))


def _conv_sample_kernel(st_ref, u_ref, w_ref, b_ref, g_ref, beta_ref, o_ref, c_ref):
    bs = u_ref.shape[0]
    w_hist = w_ref[0:CONV_K - 1, :]
    for b in range(bs):
        c_ref[b:b + 1, :] = jnp.sum(st_ref[b] * w_hist, axis=0, keepdims=True)
    acc = c_ref[...] + u_ref[...] * w_ref[CONV_K - 1:CONV_K, :] + b_ref[...]
    y = _layer_norm(acc, g_ref[...], beta_ref[...])
    o_ref[...] = (y * jax.nn.sigmoid(y)).astype(o_ref.dtype)


def _conv_sample(state, u, conv_w, conv_b, ln_g, ln_b, layer):
    b = u.shape[0]
    bs = 16
    row = lambda a: a.reshape(1, CONV_WIDTH)
    full = lambda shape: pl.BlockSpec(shape, lambda i: (0, 0))
    return pl.pallas_call(
        _conv_sample_kernel,
        out_shape=jax.ShapeDtypeStruct((b, CONV_WIDTH), BF16),
        grid=(b // bs,),
        in_specs=[pl.BlockSpec((None, bs, CONV_K - 1, CONV_WIDTH), lambda i: (layer, i, 0, 0)),
                  pl.BlockSpec((bs, CONV_WIDTH), lambda i: (i, 0)),
                  full((CONV_K, CONV_WIDTH)), full((1, CONV_WIDTH)),
                  full((1, CONV_WIDTH)), full((1, CONV_WIDTH))],
        out_specs=pl.BlockSpec((bs, CONV_WIDTH), lambda i: (i, 0)),
        scratch_shapes=[pltpu.VMEM((bs, CONV_WIDTH), F32)],
        compiler_params=_params(1, 2 * bs * 32 * CONV_WIDTH * 4 + (8 << 20)),
        name="conv_sample",
    )(state, u, conv_w, row(conv_b), row(ln_g), row(ln_b))


def _mix_kernel(a_ref, c_ref, wa_ref, wc_ref, ga_ref, gc_ref, o_ref):
    a = _dot(a_ref[...], wa_ref[...])
    c = _dot(c_ref[...], wc_ref[...])
    o_ref[...] = (ga_ref[...].astype(F32) * a + gc_ref[...].astype(F32) * c).astype(o_ref.dtype)


def _mix(attn, cact, gates, w_attn_out, w_conv_out, layer, tm):
    m = attn.shape[0]
    ng = D_MODEL // TN
    tile = pl.BlockSpec((tm, TN), lambda i, j: (i, j))
    vmem = 2 * (tm * (SLOT_WIDTH + CONV_WIDTH) * 2 + (SLOT_WIDTH + CONV_WIDTH) * TN * 2
                + 3 * tm * TN * 2) + tm * TN * 16 + (8 << 20)
    return pl.pallas_call(
        _mix_kernel,
        out_shape=jax.ShapeDtypeStruct((m, D_MODEL), BF16),
        grid=(m // tm, D_MODEL // TN),
        in_specs=[pl.BlockSpec((tm, SLOT_WIDTH), lambda i, j: (i, 0)),
                  pl.BlockSpec((tm, CONV_WIDTH), lambda i, j: (i, 0)),
                  _w_spec(SLOT_WIDTH, layer), _w_spec(CONV_WIDTH, layer), tile,
                  pl.BlockSpec((tm, TN), lambda i, j: (i, j + ng))],
        out_specs=tile,
        compiler_params=_params(2, vmem),
        name="mix",
    )(attn, cact, w_attn_out, w_conv_out, gates, gates)


def _res_ln_kernel(h_ref, w_ref, x_ref, g_ref, b_ref, o_ref, ob_ref, *scratch):
    if scratch:
        _cast_weight_once(w_ref, scratch[0])
        w = scratch[0][...]
    else:
        w = w_ref[...]
    y = DN_ALPHA * x_ref[...] + _dot(h_ref[...], w)
    y = _layer_norm(y, g_ref[...], b_ref[...])
    o_ref[...] = y
    ob_ref[...] = y.astype(ob_ref.dtype)


def _res_ln(h, w, layer, x, g, b, tm, name):
    m, k = h.shape
    cast = w.dtype != BF16
    full = lambda shape: pl.BlockSpec(shape, lambda i: (0, 0))
    w_bytes = k * D_MODEL * (6 if cast else 2)
    vmem = 2 * (tm * k * 2 + tm * D_MODEL * (4 + 4 + 2)) + w_bytes + tm * D_MODEL * 12 + (4 << 20)
    return pl.pallas_call(
        _res_ln_kernel,
        out_shape=[jax.ShapeDtypeStruct((m, D_MODEL), F32), jax.ShapeDtypeStruct((m, D_MODEL), BF16)],
        grid=(m // tm,),
        in_specs=[pl.BlockSpec((tm, k), lambda i: (i, 0)),
                  _resident_w_spec(k, D_MODEL, layer, 0),
                  pl.BlockSpec((tm, D_MODEL), lambda i: (i, 0)),
                  full((1, D_MODEL)), full((1, D_MODEL))],
        out_specs=[pl.BlockSpec((tm, D_MODEL), lambda i: (i, 0))] * 2,
        scratch_shapes=[pltpu.VMEM((k, D_MODEL), BF16)] if cast else [],
        compiler_params=_params(1, vmem),
        name=name,
    )(h, w, x, g.reshape(1, D_MODEL), b.reshape(1, D_MODEL))


def _xattn_prompt_kernel(q_ref, kv_ref, o_ref):
    for h in range(X_HEADS):
        sl = slice(h * X_HEAD_DIM, (h + 1) * X_HEAD_DIM)
        s = _dot_nt(q_ref[:, sl], kv_ref[:, sl])
        p = jnp.exp(s - jnp.max(s, axis=-1, keepdims=True))
        den = jnp.sum(p, axis=-1, keepdims=True)
        o = _dot(p.astype(BF16), kv_ref[:, D_MODEL + h * X_HEAD_DIM:D_MODEL + (h + 1) * X_HEAD_DIM])
        o_ref[:, sl] = (o * (1.0 / den)).astype(o_ref.dtype)


def _xattn_prompt(q, mem_kv, tm):
    t = q.shape[0]
    return pl.pallas_call(
        _xattn_prompt_kernel,
        out_shape=jax.ShapeDtypeStruct((t, D_MODEL), BF16),
        grid=(t // tm,),
        in_specs=[pl.BlockSpec((tm, D_MODEL), lambda i: (i, 0)),
                  pl.BlockSpec((N_MEM, 2 * D_MODEL), lambda i: (0, 0))],
        out_specs=pl.BlockSpec((tm, D_MODEL), lambda i: (i, 0)),
        compiler_params=_params(1, 32 << 20),
        name="xattn_prompt",
    )(q, mem_kv)


XATTN_KEY_CHUNK = 32


def _xattn_sample_kernel(q_ref, kv_ref, o_ref):
    bs = q_ref.shape[0]
    nl = X_HEAD_DIM // LANES
    qs = [q_ref[b] for b in range(bs)]

    def chunk(ci, carry):
        k0 = pl.multiple_of(ci * XATTN_KEY_CHUNK, XATTN_KEY_CHUNK)
        keys = pl.ds(k0, XATTN_KEY_CHUNK)
        out = []
        for b in range(bs):
            m, den, num = carry[b]
            q = qs[b]
            acc = kv_ref[b, keys, 0, :, 0:LANES] * q[None, :, 0:LANES]
            for c in range(1, nl):
                ls = slice(c * LANES, (c + 1) * LANES)
                acc = acc + kv_ref[b, keys, 0, :, ls] * q[None, :, ls]
            s = jnp.sum(acc, axis=-1, keepdims=True)
            m_new = jnp.maximum(m, jnp.max(s, axis=0))
            alpha = jnp.exp(m - m_new)
            p = jnp.exp(s - m_new[None])
            den = den * alpha + jnp.sum(p, axis=0)
            num = num * alpha + jnp.sum(p * kv_ref[b, keys, 1], axis=0)
            out.append((m_new, den, num))
        return tuple(out)

    init = tuple((jnp.full((X_HEADS, 1), MASK_VALUE, F32), jnp.zeros((X_HEADS, 1), F32),
                  jnp.zeros((X_HEADS, X_HEAD_DIM), F32)) for _ in range(bs))
    final = lax.fori_loop(0, N_MEM // XATTN_KEY_CHUNK, chunk, init)
    for b in range(bs):
        _, den, num = final[b]
        o_ref[b] = num * (1.0 / den)


def _xattn_sample(q, cache_mem_kv, layer):
    b = q.shape[0]
    bs = 2
    out = pl.pallas_call(
        _xattn_sample_kernel,
        out_shape=jax.ShapeDtypeStruct((b, X_HEADS, X_HEAD_DIM), F32),
        grid=(b // bs,),
        in_specs=[pl.BlockSpec((bs, X_HEADS, X_HEAD_DIM), lambda i: (i, 0, 0)),
                  pl.BlockSpec((None, bs, N_MEM, 2, X_HEADS, X_HEAD_DIM), lambda i: (layer, i, 0, 0, 0, 0))],
        out_specs=pl.BlockSpec((bs, X_HEADS, X_HEAD_DIM), lambda i: (i, 0, 0)),
        compiler_params=_params(1, 2 * 2 * bs * N_MEM * 2 * D_MODEL * 4 + (8 << 20)),
        name="xattn_sample",
    )(q.reshape(b, X_HEADS, X_HEAD_DIM), cache_mem_kv)
    return out.reshape(b, D_MODEL)


def _ffn_conv(h, tail, cw, cb):
    rows = lax.broadcasted_iota(jnp.int32, tail.shape, 0)
    out = cb + cw[FFN_K - 1:FFN_K] * h
    for back in range(1, FFN_K):
        shifted = pltpu.roll(h, back, 0)
        top = jnp.where(rows < back, pltpu.roll(tail, back, 0), shifted[0:SUBLANES])
        shifted = jnp.concatenate([top, shifted[SUBLANES:]], axis=0)
        out = out + cw[FFN_K - 1 - back:FFN_K - back] * shifted
    return out


def _ffn_prompt_kernel(x_ref, wg_ref, wv_ref, cwg_ref, cwv_ref, cbg_ref, cbv_ref, f_ref, lg_ref, lv_ref,
                       tg_ref, tv_ref):
    m = pl.program_id(1)
    tm = x_ref.shape[0]

    @pl.when(jnp.logical_and(pl.program_id(0) == 0, m == 0))
    def _():
        tg_ref[...] = jnp.zeros(tg_ref.shape, F32)
        tv_ref[...] = jnp.zeros(tv_ref.shape, F32)

    x = x_ref[...]
    tail_g = jnp.where(m > 0, tg_ref[...], 0.0)
    tail_v = jnp.where(m > 0, tv_ref[...], 0.0)
    new_tails, outs = [], []
    for chunk in range(TN // MXU_COLS):
        cs = slice(chunk * MXU_COLS, (chunk + 1) * MXU_COLS)
        hg = _dot(x, wg_ref[:, cs])
        hv = _dot(x, wv_ref[:, cs])
        gate = _ffn_conv(hg, tail_g[:, cs], cwg_ref[:, cs], cbg_ref[:, cs])
        val = _ffn_conv(hv, tail_v[:, cs], cwv_ref[:, cs], cbv_ref[:, cs])
        outs.append((gate * jax.nn.sigmoid(gate) * val).astype(f_ref.dtype))
        new_tails.append((hg[tm - SUBLANES:], hv[tm - SUBLANES:]))
    for chunk, (last_g, last_v) in enumerate(new_tails):
        cs = slice(chunk * MXU_COLS, (chunk + 1) * MXU_COLS)
        f_ref[:, cs] = outs[chunk]
        tg_ref[:, cs] = last_g
        tv_ref[:, cs] = last_v
        lg_ref[:, cs] = last_g
        lv_ref[:, cs] = last_v


def _ffn_prompt(x, w_up, layer, conv_w, conv_b, tm):
    t, k = x.shape
    nt = D_FF // TN
    cb = conv_b.reshape(1, 2 * D_FF)
    vmem = 2 * (tm * k * 2 + 2 * k * TN * 2 + tm * TN * 2) + tm * TN * 48 + (8 << 20)
    f, last_g, last_v = pl.pallas_call(
        _ffn_prompt_kernel,
        out_shape=[jax.ShapeDtypeStruct((t, D_FF), BF16),
                   jax.ShapeDtypeStruct((SUBLANES, D_FF), F32),
                   jax.ShapeDtypeStruct((SUBLANES, D_FF), F32)],
        grid=(nt, t // tm),
        in_specs=[pl.BlockSpec((tm, k), lambda n, m: (m, 0)),
                  pl.BlockSpec((None, k, TN), lambda n, m: (layer, 0, n)),
                  pl.BlockSpec((None, k, TN), lambda n, m: (layer, 0, n + nt)),
                  pl.BlockSpec((FFN_K, TN), lambda n, m: (0, n)),
                  pl.BlockSpec((FFN_K, TN), lambda n, m: (0, n + nt)),
                  pl.BlockSpec((1, TN), lambda n, m: (0, n)),
                  pl.BlockSpec((1, TN), lambda n, m: (0, n + nt))],
        out_specs=[pl.BlockSpec((tm, TN), lambda n, m: (m, n)),
                   pl.BlockSpec((SUBLANES, TN), lambda n, m: (0, n)),
                   pl.BlockSpec((SUBLANES, TN), lambda n, m: (0, n))],
        scratch_shapes=[pltpu.VMEM((SUBLANES, TN), F32)] * 2,
        compiler_params=_params(2, vmem),
        name="ffn_prompt",
    )(x, w_up, w_up, conv_w, conv_w, cb, cb)
    h_last = jnp.concatenate([last_g, last_v], axis=1)[SUBLANES - (FFN_K - 1):]
    return f, h_last


def _ffn_sample_kernel(x_ref, wg_ref, wv_ref, s0g_ref, s0v_ref, s1g_ref, s1v_ref,
                       cwg_ref, cwv_ref, cbg_ref, cbv_ref, f_ref, hg_ref, hv_ref):
    x = x_ref[...]

    def branch(w_ref, s0_ref, s1_ref, cw_ref, cb_ref, h_ref):
        h = _dot(x, w_ref[...])
        h_ref[...] = h
        return cb_ref[...] + cw_ref[0:1, :] * s0_ref[...] + cw_ref[1:2, :] * s1_ref[...] + cw_ref[2:3, :] * h

    gate = branch(wg_ref, s0g_ref, s1g_ref, cwg_ref, cbg_ref, hg_ref)
    val = branch(wv_ref, s0v_ref, s1v_ref, cwv_ref, cbv_ref, hv_ref)
    f_ref[...] = (gate * jax.nn.sigmoid(gate) * val).astype(f_ref.dtype)


def _ffn_sample(x, w_up, state, conv_w, conv_b, layer):
    b, k = x.shape
    nt = D_FF // TN
    st = state.reshape(state.shape[0], b, (FFN_K - 1) * 2 * D_FF)
    cb = conv_b.reshape(1, 2 * D_FF)
    st_spec = lambda off: pl.BlockSpec((None, b, TN), lambda n: (layer, 0, n + off))
    w_spec = lambda off: pl.BlockSpec((None, k, TN), lambda n: (layer, 0, n + off))
    col = lambda rows, off: pl.BlockSpec((rows, TN), lambda n: (0, n + off))
    f, hg, hv = pl.pallas_call(
        _ffn_sample_kernel,
        out_shape=[jax.ShapeDtypeStruct((b, D_FF), BF16),
                   jax.ShapeDtypeStruct((b, D_FF), F32), jax.ShapeDtypeStruct((b, D_FF), F32)],
        grid=(nt,),
        in_specs=[pl.BlockSpec((b, k), lambda n: (0, 0)),
                  w_spec(0), w_spec(nt),
                  st_spec(0), st_spec(nt), st_spec(2 * nt), st_spec(3 * nt),
                  col(FFN_K, 0), col(FFN_K, nt), col(1, 0), col(1, nt)],
        out_specs=[pl.BlockSpec((b, TN), lambda n: (0, n))] * 3,
        compiler_params=_params(1, 32 << 20),
        name="ffn_sample",
    )(x, w_up, w_up, st, st, st, st, conv_w, conv_w, cb, cb)
    return f, jnp.concatenate([hg, hv], axis=1)


def _rope_tables(pos):
    t = pos.shape[0]
    inv = ROPE_THETA ** (-jnp.arange(0, ROT_DIM, 2, dtype=F32) / ROT_DIM)
    ang = pos.astype(F32)[:, None] * inv[None, :]
    cos, sin = jnp.cos(ang), jnp.sin(ang)
    zeros = lambda n: jnp.zeros((t, n), F32)
    c = jnp.concatenate([cos, cos, jnp.ones((t, HEAD_DIM - ROT_DIM), F32)], axis=1)
    s1 = jnp.concatenate([-sin, zeros(HEAD_DIM - ROT_HALF)], axis=1)
    s2 = jnp.concatenate([zeros(ROT_HALF), sin, zeros(HEAD_DIM - ROT_DIM)], axis=1)
    return c, s1, s2


def _window_rows(k, v, g, rows):
    sl = slice(g * SLOT_WIDTH, (g + 1) * SLOT_WIDTH)
    kv = jnp.stack([k[k.shape[0] - rows:, sl], v[v.shape[0] - rows:, sl]], axis=1)
    return kv.reshape(rows, 2, HEADS_PER_GROUP, HEAD_DIM)


def kernel(x_prompt, x_sample, cache_win1_kv, cache_win2_kv, cache_win3_kv, state_conv, state_ffn_conv,
           cache_mem_kv, mem_prompt, w_in, b_gate, w_attn_out, conv_w, conv_b, conv_ln_g, conv_ln_b,
           w_conv_out, w_mix_out, ln1_g, ln1_b, w_xq, w_xkv, w_xo, ln2_g, ln2_b, w_up, ffn_conv_w,
           ffn_conv_b, w_down, ln3_g, ln3_b):
    bp, sp, _ = x_prompt.shape
    nb, ts, _ = x_sample.shape
    assert bp == 1 and ts == 1
    depth = w_in.shape[0]
    assert depth == DEPTH
    win_caches = (cache_win1_kv, cache_win2_kv, cache_win3_kv)

    tables_p = _rope_tables(jnp.arange(sp))
    tables_s = _rope_tables(jnp.full((nb,), PAST_LEN, jnp.int32))
    mem_b = mem_prompt.reshape(N_MEM, D_MODEL).astype(BF16)

    bf = lambda a: a.astype(BF16)
    w_gates = bf(w_in[:, :, OFF_GATES:])
    w_attn_out, w_conv_out = bf(w_attn_out), bf(w_conv_out)
    w_xq, w_xkv, w_up, w_down = bf(w_xq), bf(w_xkv), bf(w_up), bf(w_down)

    xp = x_prompt.reshape(sp, D_MODEL)
    xs = x_sample.reshape(nb, D_MODEL)
    xpb, xsb = bf(xp), bf(xs)
    tm_s = nb
    q_scale = HEAD_DIM ** -0.5
    xq_scale = X_HEAD_DIM ** -0.5

    win_p = [[] for _ in DIL_GROUPS]
    win_s = [[] for _ in DIL_GROUPS]
    conv_p, ffn_p, memkv_p, conv_s, ffn_s = [], [], [], [], []
    for l in range(depth):
        bg = b_gate[l].reshape(1, 2 * D_MODEL)

        mkv, mkv_b = _matmul(mem_b, w_xkv, l, [F32, BF16], N_MEM, "mem_kv")
        qs = _qkv_proj(xpb, w_in, l, OFF_Q, tables_p, True, q_scale, 512, True, [], "q_proj_p")
        *ks, k = _qkv_proj(xpb, w_in, l, OFF_K, tables_p, True, 1.0, 512, True, [F32], "k_proj_p")
        *vs, v = _qkv_proj(xpb, w_in, l, OFF_V, tables_p, False, 1.0, 512, True, [F32], "v_proj_p")
        u = _glu_proj(xpb, w_in, l, 256, "glu_proj_p")
        gates = _gate_proj(xpb, w_gates, l, bg, 1024, "gate_proj_p")
        parts = []
        for g in range(N_GROUPS):
            parts.extend(_win_attn_prompt(qs[g], ks[g], vs[g], g))
        attn = _combine_groups(parts, 512)
        cact = _conv_prompt(u, conv_w[l], conv_b[l], conv_ln_g[l], conv_ln_b[l])
        mix = _mix(attn, cact, gates, w_attn_out, w_conv_out, l, 1024)
        xp, xpb = _res_ln(mix, w_mix_out, l, xp, ln1_g[l], ln1_b[l], 256, "mix_out_ln1")
        (xq,) = _matmul(xpb, w_xq, l, [BF16], 1024, "xq_p", scale=xq_scale)
        xo = _xattn_prompt(xq, mkv_b, 512)
        xp, xpb = _res_ln(xo, w_xo, l, xp, ln2_g[l], ln2_b[l], 256, "xo_ln2")
        f, h_last = _ffn_prompt(xpb, w_up, l, ffn_conv_w[l], ffn_conv_b[l], 512)
        xp, xpb = _res_ln(f, w_down, l, xp, ln3_g[l], ln3_b[l], 256, "down_ln3")
        for g, (win, _) in enumerate(DIL_GROUPS):
            win_p[g].append(_window_rows(k, v, g, min(win, sp))[None])
        conv_p.append(u[None, sp - (CONV_K - 1):])
        ffn_p.append(h_last[None])
        memkv_p.append(mkv.reshape(1, N_MEM, 2, X_HEADS, X_HEAD_DIM))

        (q,) = _qkv_proj(xsb, w_in, l, OFF_Q, tables_s, True, q_scale, tm_s, False, [F32], "q_proj_s")
        (k,) = _qkv_proj(xsb, w_in, l, OFF_K, tables_s, True, 1.0, tm_s, False, [F32], "k_proj_s")
        (v,) = _qkv_proj(xsb, w_in, l, OFF_V, tables_s, False, 1.0, tm_s, False, [F32], "v_proj_s")
        u = _glu_proj(xsb, w_in, l, tm_s, "glu_proj_s")
        gates = _gate_proj(xsb, w_gates, l, bg, tm_s, "gate_proj_s")
        attn = bf(_win_attn_sample(q, k, v, win_caches, l))
        cact = _conv_sample(state_conv, u, conv_w[l], conv_b[l], conv_ln_g[l], conv_ln_b[l], l)
        mix = _mix(attn, cact, gates, w_attn_out, w_conv_out, l, tm_s)
        xs, xsb = _res_ln(mix, w_mix_out, l, xs, ln1_g[l], ln1_b[l], tm_s, "mix_out_ln1")
        (xq,) = _matmul(xsb, w_xq, l, [F32], tm_s, "xq_s", scale=xq_scale)
        xo = bf(_xattn_sample(xq, cache_mem_kv, l))
        xs, xsb = _res_ln(xo, w_xo, l, xs, ln2_g[l], ln2_b[l], tm_s, "xo_ln2")
        f, h_new = _ffn_sample(xsb, w_up, state_ffn_conv, ffn_conv_w[l], ffn_conv_b[l], l)
        xs, xsb = _res_ln(f, w_down, l, xs, ln3_g[l], ln3_b[l], tm_s, "down_ln3")
        for g in range(N_GROUPS):
            win_s[g].append(_window_rows(k, v, g, nb).reshape(nb, 1, 2, HEADS_PER_GROUP, HEAD_DIM))
        conv_s.append(jnp.concatenate([state_conv[l][:, 1:], u[:, None]], axis=1))
        ffn_s.append(jnp.concatenate([state_ffn_conv[l][:, 1:], h_new[:, None]], axis=1))

    return (xp.reshape(bp, sp, D_MODEL), xs.reshape(nb, ts, D_MODEL),
            jnp.stack(win_p[0]), jnp.stack(win_p[1]), jnp.stack(win_p[2]), jnp.stack(conv_p),
            jnp.stack(ffn_p), jnp.stack(memkv_p),
            jnp.stack(win_s[0]), jnp.stack(win_s[1]), jnp.stack(win_s[2]), jnp.stack(conv_s),
            jnp.stack(ffn_s))
```

```python
import functools

import jax
import jax.numpy as jnp
from jax import lax
from jax.experimental import pallas as pl
from jax.experimental.pallas import tpu as pltpu

F32 = jnp.float32
BF16 = jnp.bfloat16

D_MODEL = 2048
PAST_LEN = 2048
HEAD_DIM = 128
HEADS_PER_GROUP = 4
DIL_GROUPS = ((128, 1), (512, 4), (2048, 16))
DILATIONS = tuple(d for _, d in DIL_GROUPS)
N_GROUPS = len(DIL_GROUPS)
ATTN_HEADS = N_GROUPS * HEADS_PER_GROUP
ATTN_WIDTH = ATTN_HEADS * HEAD_DIM
SLOT_WIDTH = HEADS_PER_GROUP * HEAD_DIM
ROT_DIM = HEAD_DIM // 4
ROT_HALF = ROT_DIM // 2
ROPE_THETA = 500000.0
CONV_WIDTH = 3 * D_MODEL // 4
CONV_K = 31
N_MEM = 256
X_HEADS = 4
X_HEAD_DIM = D_MODEL // X_HEADS
D_FF = 11 * D_MODEL // 4
FFN_K = 3
LN_EPS = 1e-5
DEPTH = 2
DN_ALPHA = (2 * DEPTH) ** 0.25

OFF_Q = 0
OFF_K = ATTN_WIDTH
OFF_V = 2 * ATTN_WIDTH
OFF_A = 3 * ATTN_WIDTH
OFF_BG = OFF_A + CONV_WIDTH
OFF_GATES = OFF_BG + CONV_WIDTH

LANES = 128
SUBLANES = 8
MXU_COLS = 256
KEYS_PER_GROUP = 128
VMEM_CAP = 56 * 1024 * 1024
MASK_VALUE = -1e30
TN = 512
CAST_ROWS = 256


def _params(n_axes, vmem_bytes):
    return pltpu.CompilerParams(dimension_semantics=("arbitrary",) * n_axes,
                                vmem_limit_bytes=int(min(vmem_bytes, VMEM_CAP)))


def _layer_norm(x, g, b):
    mu = jnp.mean(x, axis=-1, keepdims=True)
    xc = x - mu
    var = jnp.mean(xc * xc, axis=-1, keepdims=True)
    return xc * lax.rsqrt(var + LN_EPS) * g + b


def _dot(a, b):
    return jnp.dot(a, b, preferred_element_type=F32)


def _dot_nt(a, b):
    return lax.dot_general(a, b, (((1,), (1,)), ((), ())), preferred_element_type=F32)


def _w_spec(k, layer, off=0):
    return pl.BlockSpec((None, k, TN), lambda i, j: (layer, 0, j + off))


def _resident_w_spec(k, n, layer, col_block):
    return pl.BlockSpec((None, k, n), lambda i: (layer, 0, col_block), pipeline_mode=pl.Buffered(1))


def _cast_weight_once(w_ref, wb_ref):
    @pl.when(pl.program_id(0) == 0)
    def _():
        def rows(i, carry):
            r = pl.ds(pl.multiple_of(i * CAST_ROWS, CAST_ROWS), CAST_ROWS)
            wb_ref[r, :] = w_ref[r, :].astype(wb_ref.dtype)
            return carry
        lax.fori_loop(0, w_ref.shape[0] // CAST_ROWS, rows, 0)


def _rope_head(xh, c, s1, s2, scale):
    r = xh * c + pltpu.roll(xh, HEAD_DIM - ROT_HALF, 1) * s1 + pltpu.roll(xh, ROT_HALF, 1) * s2
    return r * scale if scale != 1.0 else r


def _qkv_kernel(x_ref, w_ref, c_ref, s1_ref, s2_ref, *refs, rope, scale, n_perm, n_plain):
    perm_refs = refs[:n_perm]
    plain_refs = refs[n_perm:n_perm + n_plain]
    wb_ref, scr = refs[n_perm + n_plain:]
    _cast_weight_once(w_ref, wb_ref)
    x = x_ref[...]
    tm = x.shape[0]
    heads_per_chunk = MXU_COLS // HEAD_DIM
    if rope:
        c, s1, s2 = c_ref[...], s1_ref[...], s2_ref[...]
    for chunk in range(ATTN_WIDTH // MXU_COLS):
        acc = _dot(x, wb_ref[:, chunk * MXU_COLS:(chunk + 1) * MXU_COLS])
        for hc in range(heads_per_chunk):
            head = chunk * heads_per_chunk + hc
            xh = acc[:, hc * HEAD_DIM:(hc + 1) * HEAD_DIM]
            if rope:
                xh = _rope_head(xh, c, s1, s2, scale)
            sl = slice(head * HEAD_DIM, (head + 1) * HEAD_DIM)
            for o in plain_refs:
                o[:, sl] = xh.astype(o.dtype)
            if n_perm:
                scr[head] = xh
    for g in range(n_perm):
        dil = DILATIONS[g]
        o = perm_refs[g]
        for h in range(HEADS_PER_GROUP):
            head = g * HEADS_PER_GROUP + h
            sl = slice(h * HEAD_DIM, (h + 1) * HEAD_DIM)
            for r in range(dil):
                rows = scr[head] if dil == 1 else scr[head, pl.ds(r, tm // dil, stride=dil), :]
                o[r, :, sl] = rows.astype(o.dtype)


def _qkv_proj(x, w_in, layer, col_off, tables, rope, scale, tm, perm, plain_dtypes, name):
    m, k = x.shape
    n_perm = N_GROUPS if perm else 0
    tab_spec = pl.BlockSpec((tm, HEAD_DIM), lambda i: (i, 0))
    perm_shapes = [jax.ShapeDtypeStruct((d, m // d, SLOT_WIDTH), BF16) for d in DILATIONS[:n_perm]]
    perm_specs = [pl.BlockSpec((d, tm // d, SLOT_WIDTH), lambda i: (0, i, 0)) for d in DILATIONS[:n_perm]]
    plain_shapes = [jax.ShapeDtypeStruct((m, ATTN_WIDTH), d) for d in plain_dtypes]
    plain_specs = [pl.BlockSpec((tm, ATTN_WIDTH), lambda i: (i, 0)) for _ in plain_dtypes]
    vmem = (2 * tm * k * 2 + k * ATTN_WIDTH * 6 + 2 * 3 * tm * HEAD_DIM * 4 + tm * ATTN_WIDTH * 4
            + 2 * n_perm * tm * SLOT_WIDTH * 2
            + 2 * sum(tm * ATTN_WIDTH * jnp.dtype(d).itemsize for d in plain_dtypes)
            + tm * MXU_COLS * 32 + (6 << 20))
    return pl.pallas_call(
        functools.partial(_qkv_kernel, rope=rope, scale=scale, n_perm=n_perm, n_plain=len(plain_dtypes)),
        out_shape=perm_shapes + plain_shapes,
        grid=(m // tm,),
        in_specs=[pl.BlockSpec((tm, k), lambda i: (i, 0)),
                  _resident_w_spec(k, ATTN_WIDTH, layer, col_off // ATTN_WIDTH),
                  tab_spec, tab_spec, tab_spec],
        out_specs=perm_specs + plain_specs,
        scratch_shapes=[pltpu.VMEM((k, ATTN_WIDTH), BF16), pltpu.VMEM((ATTN_HEADS, tm, HEAD_DIM), F32)],
        compiler_params=_params(1, vmem),
        name=name,
    )(x, w_in, *tables)


def _glu_kernel(x_ref, wa_ref, wb_ref, o_ref, wa_scr, wb_scr):
    _cast_weight_once(wa_ref, wa_scr)
    _cast_weight_once(wb_ref, wb_scr)
    x = x_ref[...]
    for chunk in range(CONV_WIDTH // MXU_COLS):
        cs = slice(chunk * MXU_COLS, (chunk + 1) * MXU_COLS)
        o_ref[:, cs] = _dot(x, wa_scr[:, cs]) * jax.nn.sigmoid(_dot(x, wb_scr[:, cs]))


def _glu_proj(x, w_in, layer, tm, name):
    m, k = x.shape
    vmem = 2 * tm * k * 2 + 2 * k * CONV_WIDTH * 6 + 2 * tm * CONV_WIDTH * 4 + tm * MXU_COLS * 32 + (4 << 20)
    return pl.pallas_call(
        _glu_kernel,
        out_shape=jax.ShapeDtypeStruct((m, CONV_WIDTH), F32),
        grid=(m // tm,),
        in_specs=[pl.BlockSpec((tm, k), lambda i: (i, 0)),
                  _resident_w_spec(k, CONV_WIDTH, layer, OFF_A // CONV_WIDTH),
                  _resident_w_spec(k, CONV_WIDTH, layer, OFF_BG // CONV_WIDTH)],
        out_specs=pl.BlockSpec((tm, CONV_WIDTH), lambda i: (i, 0)),
        scratch_shapes=[pltpu.VMEM((k, CONV_WIDTH), BF16)] * 2,
        compiler_params=_params(1, vmem),
        name=name,
    )(x, w_in, w_in)


def _gate_kernel(x_ref, w_ref, b_ref, o_ref):
    x = x_ref[...]
    for chunk in range(TN // MXU_COLS):
        cs = slice(chunk * MXU_COLS, (chunk + 1) * MXU_COLS)
        o_ref[:, cs] = jax.nn.sigmoid(_dot(x, w_ref[:, cs]) + b_ref[:, cs]).astype(o_ref.dtype)


def _gate_proj(x, w_gates, layer, b_gate, tm, name):
    m, k = x.shape
    vmem = 2 * (tm * k * 2 + k * TN * 2 + tm * TN * 2) + tm * TN * 16
    return pl.pallas_call(
        _gate_kernel,
        out_shape=jax.ShapeDtypeStruct((m, 2 * D_MODEL), BF16),
        grid=(m // tm, 2 * D_MODEL // TN),
        in_specs=[pl.BlockSpec((tm, k), lambda i, j: (i, 0)), _w_spec(k, layer),
                  pl.BlockSpec((1, TN), lambda i, j: (0, j))],
        out_specs=pl.BlockSpec((tm, TN), lambda i, j: (i, j)),
        compiler_params=_params(2, vmem + (8 << 20)),
        name=name,
    )(x, w_gates, b_gate)


def _mm_kernel(x_ref, w_ref, *out_refs, scale):
    acc = _dot(x_ref[...], w_ref[...])
    if scale != 1.0:
        acc = acc * scale
    for o in out_refs:
        o[...] = acc.astype(o.dtype)


def _matmul(x, w, layer, out_dtypes, tm, name, scale=1.0):
    m, k = x.shape
    n_cols = w.shape[2]
    vmem = 2 * (tm * k * 2 + k * TN * 2
                + sum(tm * TN * jnp.dtype(d).itemsize for d in out_dtypes)) + tm * TN * 8
    return pl.pallas_call(
        functools.partial(_mm_kernel, scale=scale),
        out_shape=[jax.ShapeDtypeStruct((m, n_cols), d) for d in out_dtypes],
        grid=(m // tm, n_cols // TN),
        in_specs=[pl.BlockSpec((tm, k), lambda i, j: (i, 0)), _w_spec(k, layer)],
        out_specs=[pl.BlockSpec((tm, TN), lambda i, j: (i, j)) for _ in out_dtypes],
        compiler_params=_params(2, vmem + (8 << 20)),
        name=name,
    )(x, w)


WIN_BLOCKS_PER_STEP = 2


def _win_attn_kernel(q_ref, kc_ref, kp_ref, vc_ref, vp_ref, o_ref, l_ref):
    i = pl.program_id(1)
    tq = KEYS_PER_GROUP
    row = lax.broadcasted_iota(jnp.int32, (tq, tq), 0)
    col = lax.broadcasted_iota(jnp.int32, (tq, tq), 1)
    mask_c = col <= row
    mask_p = col >= row
    mask_first = jnp.logical_and(mask_p, i > 0)
    for blk in range(q_ref.shape[0] // tq):
        rows = slice(blk * tq, (blk + 1) * tq)
        for h in range(HEADS_PER_GROUP):
            sl = slice(h * HEAD_DIM, (h + 1) * HEAD_DIM)
            q = q_ref[rows, sl]
            if blk == 0:
                k_prev, v_prev, mask_prev = kp_ref[:, sl], vp_ref[:, sl], mask_first
            else:
                before = slice((blk - 1) * tq, blk * tq)
                k_prev, v_prev, mask_prev = kc_ref[before, sl], vc_ref[before, sl], mask_p
            sc = jnp.where(mask_c, _dot_nt(q, kc_ref[rows, sl]), MASK_VALUE)
            sp = jnp.where(mask_prev, _dot_nt(q, k_prev), MASK_VALUE)
            m = jnp.maximum(jnp.max(sc, axis=-1, keepdims=True), jnp.max(sp, axis=-1, keepdims=True))
            pc = jnp.exp(sc - m)
            pp = jnp.exp(sp - m)
            den = jnp.sum(pc, axis=-1, keepdims=True) + jnp.sum(pp, axis=-1, keepdims=True)
            o = _dot(pc.astype(BF16), vc_ref[rows, sl]) + _dot(pp.astype(BF16), v_prev)
            o_ref[rows, sl] = o * (1.0 / den)
            l_ref[rows, sl] = jnp.broadcast_to(m + jnp.log(den), (tq, HEAD_DIM))


def _win_attn_prompt(q, k, v, g):
    dil, tc, _ = q.shape
    tq = WIN_BLOCKS_PER_STEP * KEYS_PER_GROUP
    cur = pl.BlockSpec((None, tq, SLOT_WIDTH), lambda r, i: (r, i, 0))
    prev = pl.BlockSpec((None, KEYS_PER_GROUP, SLOT_WIDTH),
                        lambda r, i: (r, jnp.maximum(i * WIN_BLOCKS_PER_STEP - 1, 0), 0))
    return pl.pallas_call(
        _win_attn_kernel,
        out_shape=[jax.ShapeDtypeStruct((dil, tc, SLOT_WIDTH), F32)] * 2,
        grid=(dil, tc // tq),
        in_specs=[cur, cur, prev, cur, prev],
        out_specs=[cur, cur],
        compiler_params=_params(2, 24 << 20),
        name=f"win_attn_g{g}",
    )(q, k, k, v, v)


def _combine_kernel(o1, l1, o2, l2, o3, l3, out_ref, so2, sl2, so3, sl3):
    tm = out_ref.shape[0]
    lane_blocks = SLOT_WIDTH // LANES
    for src, dst, dil in ((o2, so2, DILATIONS[1]), (l2, sl2, DILATIONS[1]),
                          (o3, so3, DILATIONS[2]), (l3, sl3, DILATIONS[2])):
        for c in range(lane_blocks):
            for r in range(dil):
                dst[c, pl.ds(r, tm // dil, stride=dil), :] = src[r, :, c * LANES:(c + 1) * LANES]
    for c in range(lane_blocks):
        ls = slice(c * LANES, (c + 1) * LANES)
        la, lb, lc = l1[0, :, ls], sl2[c], sl3[c]
        m = jnp.maximum(jnp.maximum(la, lb), lc)
        ea, eb, ec = jnp.exp(la - m), jnp.exp(lb - m), jnp.exp(lc - m)
        num = ea * o1[0, :, ls] + eb * so2[c] + ec * so3[c]
        out_ref[:, ls] = (num * (1.0 / (ea + eb + ec))).astype(out_ref.dtype)


def _combine_groups(parts, tm):
    t = parts[0].shape[1]
    specs = []
    for dil in DILATIONS:
        specs += [pl.BlockSpec((dil, tm // dil, SLOT_WIDTH), lambda i: (0, i, 0))] * 2
    return pl.pallas_call(
        _combine_kernel,
        out_shape=jax.ShapeDtypeStruct((t, SLOT_WIDTH), BF16),
        grid=(t // tm,),
        in_specs=specs,
        out_specs=pl.BlockSpec((tm, SLOT_WIDTH), lambda i: (i, 0)),
        scratch_shapes=[pltpu.VMEM((SLOT_WIDTH // LANES, tm, LANES), F32)] * 4,
        compiler_params=_params(1, 32 << 20),
        name="combine_groups",
    )(*parts)


KV_ROWS = 2 * HEADS_PER_GROUP


def _win_decode_kernel(q_ref, kvn_ref, c1_ref, c2_ref, c3_ref, o_ref):
    caches = (c1_ref, c2_ref, c3_ref)
    bs = q_ref.shape[0]

    def body(b, carry):
        q8 = q_ref[b]
        scores = [jnp.sum(kvn_ref[b] * q8, axis=-1, keepdims=True)]
        for g in range(N_GROUPS):
            scores.append(jnp.sum(caches[g][b] * q8[g][None], axis=-1, keepdims=True))
        m = jnp.max(scores[0], axis=0)
        for s in scores[1:]:
            m = jnp.maximum(m, jnp.max(s, axis=0))
        num = jnp.zeros((KV_ROWS, HEAD_DIM), F32)
        den = jnp.zeros((KV_ROWS, 1), F32)
        for s, rows in zip(scores, [kvn_ref[b]] + [c[b] for c in caches]):
            p = jnp.exp(s - m[None])
            den = den + jnp.sum(p, axis=0)
            p_val = pltpu.roll(jnp.broadcast_to(p, rows.shape), HEADS_PER_GROUP, 1)
            num = num + jnp.sum(p_val * rows, axis=0)
        o_ref[b] = num * (1.0 / pltpu.roll(den, HEADS_PER_GROUP, 0))
        return carry

    lax.fori_loop(0, bs, body, 0, unroll=2)


def _win_attn_sample(q, k_new, v_new, caches, layer):
    b = q.shape[0]
    bs = 4
    views, specs = [], []
    for cache, (win, dil) in zip(caches, DIL_GROUPS):
        depth, _, n_past = cache.shape[:3]
        assert n_past == win and n_past % dil == 0
        views.append(cache.reshape(depth, b, n_past // dil, dil, KV_ROWS, HEAD_DIM))
        specs.append(pl.BlockSpec((None, bs, KEYS_PER_GROUP, None, KV_ROWS, HEAD_DIM),
                                  lambda i: (layer, i, 0, 0, 0, 0)))
    heads = lambda a: a.reshape(b, N_GROUPS, HEADS_PER_GROUP, HEAD_DIM)
    q8 = jnp.concatenate([heads(q), heads(q)], axis=2)
    kv8 = jnp.concatenate([heads(k_new), heads(v_new)], axis=2)
    row_spec = pl.BlockSpec((bs, N_GROUPS, KV_ROWS, HEAD_DIM), lambda i: (i, 0, 0, 0))
    vmem = 2 * (3 * bs * KEYS_PER_GROUP * KV_ROWS * HEAD_DIM * 4) + (16 << 20)
    out = pl.pallas_call(
        _win_decode_kernel,
        out_shape=jax.ShapeDtypeStruct((b, KV_ROWS, HEAD_DIM), F32),
        grid=(b // bs,),
        in_specs=[row_spec, row_spec] + specs,
        out_specs=pl.BlockSpec((bs, KV_ROWS, HEAD_DIM), lambda i: (i, 0, 0)),
        compiler_params=_params(1, vmem),
        name="win_attn_sample",
    )(q8, kv8, *views)
    return out[:, HEADS_PER_GROUP:].reshape(b, SLOT_WIDTH)


CONV_HALO = 32
CONV_ROWS = 64
LN_ROWS = 16


def _conv_prompt_kernel(u_ref, halo_ref, w_ref, b_ref, g_ref, beta_ref, o_ref, ext_ref, sh_ref, c_ref):
    i = pl.program_id(0)
    tt = u_ref.shape[0]
    first_tap = CONV_HALO - (CONV_K - 1)
    halo = halo_ref[...]
    ext_ref[0:CONV_HALO, :] = jnp.where(i > 0, halo, jnp.zeros_like(halo))
    ext_ref[CONV_HALO:, :] = u_ref[...]
    n_sh = tt + CONV_HALO - SUBLANES
    for s in range(1, SUBLANES):
        sh_ref[s - 1] = ext_ref[pl.ds(s, n_sh), :]

    n_tiles = CONV_ROWS // SUBLANES

    def tap_rows(ci, carry):
        r0 = pl.multiple_of(ci * CONV_ROWS, CONV_ROWS)
        for lb in range(CONV_WIDTH // LANES):
            ls = slice(lb * LANES, (lb + 1) * LANES)
            bias = jnp.broadcast_to(b_ref[:, ls], (SUBLANES, LANES))
            acc = [bias] * n_tiles
            for s in range(SUBLANES):
                taps = [(a, a * SUBLANES + s - first_tap) for a in range((CONV_HALO + SUBLANES) // SUBLANES)
                        if 0 <= a * SUBLANES + s - first_tap < CONV_K]
                a_lo, a_hi = taps[0][0], taps[-1][0]
                rows = pl.ds(r0 + a_lo * SUBLANES, CONV_ROWS + (a_hi - a_lo) * SUBLANES)
                strip = ext_ref[rows, ls] if s == 0 else sh_ref[s - 1, rows, ls]
                for a, j in taps:
                    wj = w_ref[j:j + 1, ls]
                    for k in range(n_tiles):
                        t0 = (k + a - a_lo) * SUBLANES
                        acc[k] = acc[k] + strip[t0:t0 + SUBLANES] * wj
            c_ref[pl.ds(r0, CONV_ROWS), ls] = jnp.concatenate(acc, axis=0)
        return carry

    lax.fori_loop(0, tt // CONV_ROWS, tap_rows, 0)

    def ln_rows(ci, carry):
        r0 = pl.multiple_of(ci * LN_ROWS, LN_ROWS)
        y = _layer_norm(c_ref[pl.ds(r0, LN_ROWS), :], g_ref[...], beta_ref[...])
        o_ref[pl.ds(r0, LN_ROWS), :] = (y * jax.nn.sigmoid(y)).astype(o_ref.dtype)
        return carry

    lax.fori_loop(0, tt // LN_ROWS, ln_rows, 0, unroll=4)


def _conv_prompt(u, conv_w, conv_b, ln_g, ln_b):
    t = u.shape[0]
    tt = 256
    row = lambda a: a.reshape(1, CONV_WIDTH)
    full = lambda shape: pl.BlockSpec(shape, lambda i: (0, 0))
    hb = tt // CONV_HALO
    vmem = (2 * (tt + CONV_HALO) * CONV_WIDTH * 4 + 2 * tt * CONV_WIDTH * 2
            + (8 * (tt + CONV_HALO) + tt) * CONV_WIDTH * 4 + (8 << 20))
    return pl.pallas_call(
        _conv_prompt_kernel,
        out_shape=jax.ShapeDtypeStruct((t, CONV_WIDTH), BF16),
        grid=(t // tt,),
        in_specs=[pl.BlockSpec((tt, CONV_WIDTH), lambda i: (i, 0)),
                  pl.BlockSpec((CONV_HALO, CONV_WIDTH), lambda i: (jnp.maximum(i * hb - 1, 0), 0)),
                  full((CONV_K, CONV_WIDTH)), full((1, CONV_WIDTH)),
                  full((1, CONV_WIDTH)), full((1, CONV_WIDTH))],
        out_specs=pl.BlockSpec((tt, CONV_WIDTH), lambda i: (i, 0)),
        scratch_shapes=[pltpu.VMEM((tt + CONV_HALO, CONV_WIDTH), F32),
                        pltpu.VMEM((SUBLANES - 1, tt + CONV_HALO - SUBLANES, CONV_WIDTH), F32),
                        pltpu.VMEM((tt, CONV_WIDTH), F32)],
        compiler_params=_params(1, vmem),
        name="conv_prompt",
    )(u, u, conv_w, row(conv_b), row(ln_g), row(ln_b))


def _conv_sample_kernel(st_ref, u_ref, w_ref, b_ref, g_ref, beta_ref, o_ref, ns_ref, c_ref):
    bs = u_ref.shape[0]
    hist = CONV_K - 1
    w_hist = w_ref[0:hist, :]
    for b in range(bs):
        st = st_ref[b]
        c_ref[b:b + 1, :] = jnp.sum(st * w_hist, axis=0, keepdims=True)
        ns_ref[b, 0:hist - 1, :] = st[1:hist]
        ns_ref[b, hist - 1:hist, :] = u_ref[b:b + 1, :]
    acc = c_ref[...] + u_ref[...] * w_ref[CONV_K - 1:CONV_K, :] + b_ref[...]
    y = _layer_norm(acc, g_ref[...], beta_ref[...])
    o_ref[...] = (y * jax.nn.sigmoid(y)).astype(o_ref.dtype)


def _conv_sample(state, u, conv_w, conv_b, ln_g, ln_b, layer):
    b = u.shape[0]
    bs = 16
    row = lambda a: a.reshape(1, CONV_WIDTH)
    full = lambda shape: pl.BlockSpec(shape, lambda i: (0, 0))
    return pl.pallas_call(
        _conv_sample_kernel,
        out_shape=[jax.ShapeDtypeStruct((b, CONV_WIDTH), BF16),
                   jax.ShapeDtypeStruct((b, CONV_K - 1, CONV_WIDTH), F32)],
        grid=(b // bs,),
        in_specs=[pl.BlockSpec((None, bs, CONV_K - 1, CONV_WIDTH), lambda i: (layer, i, 0, 0)),
                  pl.BlockSpec((bs, CONV_WIDTH), lambda i: (i, 0)),
                  full((CONV_K, CONV_WIDTH)), full((1, CONV_WIDTH)),
                  full((1, CONV_WIDTH)), full((1, CONV_WIDTH))],
        out_specs=[pl.BlockSpec((bs, CONV_WIDTH), lambda i: (i, 0)),
                   pl.BlockSpec((bs, CONV_K - 1, CONV_WIDTH), lambda i: (i, 0, 0))],
        scratch_shapes=[pltpu.VMEM((bs, CONV_WIDTH), F32)],
        compiler_params=_params(1, 4 * bs * 32 * CONV_WIDTH * 4 + (8 << 20)),
        name="conv_sample",
    )(state, u, conv_w, row(conv_b), row(ln_g), row(ln_b))


def _mix_kernel(a_ref, c_ref, wa_ref, wc_ref, ga_ref, gc_ref, o_ref):
    a = _dot(a_ref[...], wa_ref[...])
    c = _dot(c_ref[...], wc_ref[...])
    o_ref[...] = (ga_ref[...].astype(F32) * a + gc_ref[...].astype(F32) * c).astype(o_ref.dtype)


def _mix(attn, cact, gates, w_attn_out, w_conv_out, layer, tm):
    m = attn.shape[0]
    ng = D_MODEL // TN
    tile = pl.BlockSpec((tm, TN), lambda i, j: (i, j))
    vmem = 2 * (tm * (SLOT_WIDTH + CONV_WIDTH) * 2 + (SLOT_WIDTH + CONV_WIDTH) * TN * 2
                + 3 * tm * TN * 2) + tm * TN * 16 + (8 << 20)
    return pl.pallas_call(
        _mix_kernel,
        out_shape=jax.ShapeDtypeStruct((m, D_MODEL), BF16),
        grid=(m // tm, D_MODEL // TN),
        in_specs=[pl.BlockSpec((tm, SLOT_WIDTH), lambda i, j: (i, 0)),
                  pl.BlockSpec((tm, CONV_WIDTH), lambda i, j: (i, 0)),
                  _w_spec(SLOT_WIDTH, layer), _w_spec(CONV_WIDTH, layer), tile,
                  pl.BlockSpec((tm, TN), lambda i, j: (i, j + ng))],
        out_specs=tile,
        compiler_params=_params(2, vmem),
        name="mix",
    )(attn, cact, w_attn_out, w_conv_out, gates, gates)


def _res_ln_kernel(h_ref, w_ref, x_ref, g_ref, b_ref, o_ref, ob_ref, *scratch):
    if scratch:
        _cast_weight_once(w_ref, scratch[0])
        w = scratch[0][...]
    else:
        w = w_ref[...]
    y = DN_ALPHA * x_ref[...] + _dot(h_ref[...], w)
    y = _layer_norm(y, g_ref[...], b_ref[...])
    o_ref[...] = y
    ob_ref[...] = y.astype(ob_ref.dtype)


def _res_ln(h, w, layer, x, g, b, tm, name):
    m, k = h.shape
    cast = w.dtype != BF16
    full = lambda shape: pl.BlockSpec(shape, lambda i: (0, 0))
    w_bytes = k * D_MODEL * (6 if cast else 2)
    vmem = 2 * (tm * k * 2 + tm * D_MODEL * (4 + 4 + 2)) + w_bytes + tm * D_MODEL * 12 + (4 << 20)
    return pl.pallas_call(
        _res_ln_kernel,
        out_shape=[jax.ShapeDtypeStruct((m, D_MODEL), F32), jax.ShapeDtypeStruct((m, D_MODEL), BF16)],
        grid=(m // tm,),
        in_specs=[pl.BlockSpec((tm, k), lambda i: (i, 0)),
                  _resident_w_spec(k, D_MODEL, layer, 0),
                  pl.BlockSpec((tm, D_MODEL), lambda i: (i, 0)),
                  full((1, D_MODEL)), full((1, D_MODEL))],
        out_specs=[pl.BlockSpec((tm, D_MODEL), lambda i: (i, 0))] * 2,
        scratch_shapes=[pltpu.VMEM((k, D_MODEL), BF16)] if cast else [],
        compiler_params=_params(1, vmem),
        name=name,
    )(h, w, x, g.reshape(1, D_MODEL), b.reshape(1, D_MODEL))


def _xattn_prompt_kernel(q_ref, kv_ref, o_ref):
    for h in range(X_HEADS):
        sl = slice(h * X_HEAD_DIM, (h + 1) * X_HEAD_DIM)
        s = _dot_nt(q_ref[:, sl], kv_ref[:, sl])
        p = jnp.exp(s - jnp.max(s, axis=-1, keepdims=True))
        den = jnp.sum(p, axis=-1, keepdims=True)
        o = _dot(p.astype(BF16), kv_ref[:, D_MODEL + h * X_HEAD_DIM:D_MODEL + (h + 1) * X_HEAD_DIM])
        o_ref[:, sl] = (o * (1.0 / den)).astype(o_ref.dtype)


def _xattn_prompt(q, mem_kv, tm):
    t = q.shape[0]
    return pl.pallas_call(
        _xattn_prompt_kernel,
        out_shape=jax.ShapeDtypeStruct((t, D_MODEL), BF16),
        grid=(t // tm,),
        in_specs=[pl.BlockSpec((tm, D_MODEL), lambda i: (i, 0)),
                  pl.BlockSpec((N_MEM, 2 * D_MODEL), lambda i: (0, 0))],
        out_specs=pl.BlockSpec((tm, D_MODEL), lambda i: (i, 0)),
        compiler_params=_params(1, 32 << 20),
        name="xattn_prompt",
    )(q, mem_kv)


XATTN_KEY_CHUNK = 32


def _xattn_sample_kernel(q_ref, kv_ref, o_ref):
    bs = q_ref.shape[0]
    nl = X_HEAD_DIM // LANES
    qs = [q_ref[b] for b in range(bs)]

    def chunk(ci, carry):
        k0 = pl.multiple_of(ci * XATTN_KEY_CHUNK, XATTN_KEY_CHUNK)
        keys = pl.ds(k0, XATTN_KEY_CHUNK)
        out = []
        for b in range(bs):
            m, den, num = carry[b]
            q = qs[b]
            acc = kv_ref[b, keys, 0, :, 0:LANES] * q[None, :, 0:LANES]
            for c in range(1, nl):
                ls = slice(c * LANES, (c + 1) * LANES)
                acc = acc + kv_ref[b, keys, 0, :, ls] * q[None, :, ls]
            s = jnp.sum(acc, axis=-1, keepdims=True)
            m_new = jnp.maximum(m, jnp.max(s, axis=0))
            alpha = jnp.exp(m - m_new)
            p = jnp.exp(s - m_new[None])
            den = den * alpha + jnp.sum(p, axis=0)
            num = num * alpha + jnp.sum(p * kv_ref[b, keys, 1], axis=0)
            out.append((m_new, den, num))
        return tuple(out)

    init = tuple((jnp.full((X_HEADS, 1), MASK_VALUE, F32), jnp.zeros((X_HEADS, 1), F32),
                  jnp.zeros((X_HEADS, X_HEAD_DIM), F32)) for _ in range(bs))
    final = lax.fori_loop(0, N_MEM // XATTN_KEY_CHUNK, chunk, init)
    for b in range(bs):
        _, den, num = final[b]
        o_ref[b] = num * (1.0 / den)


def _xattn_sample(q, cache_mem_kv, layer):
    b = q.shape[0]
    bs = 2
    out = pl.pallas_call(
        _xattn_sample_kernel,
        out_shape=jax.ShapeDtypeStruct((b, X_HEADS, X_HEAD_DIM), F32),
        grid=(b // bs,),
        in_specs=[pl.BlockSpec((bs, X_HEADS, X_HEAD_DIM), lambda i: (i, 0, 0)),
                  pl.BlockSpec((None, bs, N_MEM, 2, X_HEADS, X_HEAD_DIM), lambda i: (layer, i, 0, 0, 0, 0))],
        out_specs=pl.BlockSpec((bs, X_HEADS, X_HEAD_DIM), lambda i: (i, 0, 0)),
        compiler_params=_params(1, 2 * 2 * bs * N_MEM * 2 * D_MODEL * 4 + (8 << 20)),
        name="xattn_sample",
    )(q.reshape(b, X_HEADS, X_HEAD_DIM), cache_mem_kv)
    return out.reshape(b, D_MODEL)


def _ffn_conv(h, tail, cw, cb):
    rows = lax.broadcasted_iota(jnp.int32, tail.shape, 0)
    out = cb + cw[FFN_K - 1:FFN_K] * h
    for back in range(1, FFN_K):
        shifted = pltpu.roll(h, back, 0)
        top = jnp.where(rows < back, pltpu.roll(tail, back, 0), shifted[0:SUBLANES])
        shifted = jnp.concatenate([top, shifted[SUBLANES:]], axis=0)
        out = out + cw[FFN_K - 1 - back:FFN_K - back] * shifted
    return out


def _ffn_prompt_kernel(x_ref, wg_ref, wv_ref, cwg_ref, cwv_ref, cbg_ref, cbv_ref, f_ref, lg_ref, lv_ref,
                       tg_ref, tv_ref):
    m = pl.program_id(1)
    tm = x_ref.shape[0]

    @pl.when(jnp.logical_and(pl.program_id(0) == 0, m == 0))
    def _():
        tg_ref[...] = jnp.zeros(tg_ref.shape, F32)
        tv_ref[...] = jnp.zeros(tv_ref.shape, F32)

    x = x_ref[...]
    tail_g = jnp.where(m > 0, tg_ref[...], 0.0)
    tail_v = jnp.where(m > 0, tv_ref[...], 0.0)
    new_tails, outs = [], []
    for chunk in range(TN // MXU_COLS):
        cs = slice(chunk * MXU_COLS, (chunk + 1) * MXU_COLS)
        hg = _dot(x, wg_ref[:, cs])
        hv = _dot(x, wv_ref[:, cs])
        gate = _ffn_conv(hg, tail_g[:, cs], cwg_ref[:, cs], cbg_ref[:, cs])
        val = _ffn_conv(hv, tail_v[:, cs], cwv_ref[:, cs], cbv_ref[:, cs])
        outs.append((gate * jax.nn.sigmoid(gate) * val).astype(f_ref.dtype))
        new_tails.append((hg[tm - SUBLANES:], hv[tm - SUBLANES:]))
    for chunk, (last_g, last_v) in enumerate(new_tails):
        cs = slice(chunk * MXU_COLS, (chunk + 1) * MXU_COLS)
        f_ref[:, cs] = outs[chunk]
        tg_ref[:, cs] = last_g
        tv_ref[:, cs] = last_v
        lg_ref[:, cs] = last_g
        lv_ref[:, cs] = last_v


def _ffn_prompt(x, w_up, layer, conv_w, conv_b, tm):
    t, k = x.shape
    nt = D_FF // TN
    cb = conv_b.reshape(1, 2 * D_FF)
    vmem = 2 * (tm * k * 2 + 2 * k * TN * 2 + tm * TN * 2) + tm * TN * 48 + (8 << 20)
    f, last_g, last_v = pl.pallas_call(
        _ffn_prompt_kernel,
        out_shape=[jax.ShapeDtypeStruct((t, D_FF), BF16),
                   jax.ShapeDtypeStruct((SUBLANES, D_FF), F32),
                   jax.ShapeDtypeStruct((SUBLANES, D_FF), F32)],
        grid=(nt, t // tm),
        in_specs=[pl.BlockSpec((tm, k), lambda n, m: (m, 0)),
                  pl.BlockSpec((None, k, TN), lambda n, m: (layer, 0, n)),
                  pl.BlockSpec((None, k, TN), lambda n, m: (layer, 0, n + nt)),
                  pl.BlockSpec((FFN_K, TN), lambda n, m: (0, n)),
                  pl.BlockSpec((FFN_K, TN), lambda n, m: (0, n + nt)),
                  pl.BlockSpec((1, TN), lambda n, m: (0, n)),
                  pl.BlockSpec((1, TN), lambda n, m: (0, n + nt))],
        out_specs=[pl.BlockSpec((tm, TN), lambda n, m: (m, n)),
                   pl.BlockSpec((SUBLANES, TN), lambda n, m: (0, n)),
                   pl.BlockSpec((SUBLANES, TN), lambda n, m: (0, n))],
        scratch_shapes=[pltpu.VMEM((SUBLANES, TN), F32)] * 2,
        compiler_params=_params(2, vmem),
        name="ffn_prompt",
    )(x, w_up, w_up, conv_w, conv_w, cb, cb)
    h_last = jnp.concatenate([last_g, last_v], axis=1)[SUBLANES - (FFN_K - 1):]
    return f, h_last


def _ffn_sample_kernel(x_ref, wg_ref, wv_ref, s0g_ref, s0v_ref, s1g_ref, s1v_ref,
                       cwg_ref, cwv_ref, cbg_ref, cbv_ref, f_ref, hg_ref, hv_ref):
    x = x_ref[...]

    def branch(w_ref, s0_ref, s1_ref, cw_ref, cb_ref, h_ref):
        h = _dot(x, w_ref[...])
        h_ref[...] = h
        return cb_ref[...] + cw_ref[0:1, :] * s0_ref[...] + cw_ref[1:2, :] * s1_ref[...] + cw_ref[2:3, :] * h

    gate = branch(wg_ref, s0g_ref, s1g_ref, cwg_ref, cbg_ref, hg_ref)
    val = branch(wv_ref, s0v_ref, s1v_ref, cwv_ref, cbv_ref, hv_ref)
    f_ref[...] = (gate * jax.nn.sigmoid(gate) * val).astype(f_ref.dtype)


def _ffn_sample(x, w_up, state, conv_w, conv_b, layer):
    b, k = x.shape
    nt = D_FF // TN
    st = state.reshape(state.shape[0], b, (FFN_K - 1) * 2 * D_FF)
    cb = conv_b.reshape(1, 2 * D_FF)
    st_spec = lambda off: pl.BlockSpec((None, b, TN), lambda n: (layer, 0, n + off))
    w_spec = lambda off: pl.BlockSpec((None, k, TN), lambda n: (layer, 0, n + off))
    col = lambda rows, off: pl.BlockSpec((rows, TN), lambda n: (0, n + off))
    f, hg, hv = pl.pallas_call(
        _ffn_sample_kernel,
        out_shape=[jax.ShapeDtypeStruct((b, D_FF), BF16),
                   jax.ShapeDtypeStruct((b, D_FF), F32), jax.ShapeDtypeStruct((b, D_FF), F32)],
        grid=(nt,),
        in_specs=[pl.BlockSpec((b, k), lambda n: (0, 0)),
                  w_spec(0), w_spec(nt),
                  st_spec(0), st_spec(nt), st_spec(2 * nt), st_spec(3 * nt),
                  col(FFN_K, 0), col(FFN_K, nt), col(1, 0), col(1, nt)],
        out_specs=[pl.BlockSpec((b, TN), lambda n: (0, n))] * 3,
        compiler_params=_params(1, 32 << 20),
        name="ffn_sample",
    )(x, w_up, w_up, st, st, st, st, conv_w, conv_w, cb, cb)
    return f, jnp.concatenate([hg, hv], axis=1)


def _rope_tables(pos):
    t = pos.shape[0]
    inv = ROPE_THETA ** (-jnp.arange(0, ROT_DIM, 2, dtype=F32) / ROT_DIM)
    ang = pos.astype(F32)[:, None] * inv[None, :]
    cos, sin = jnp.cos(ang), jnp.sin(ang)
    zeros = lambda n: jnp.zeros((t, n), F32)
    c = jnp.concatenate([cos, cos, jnp.ones((t, HEAD_DIM - ROT_DIM), F32)], axis=1)
    s1 = jnp.concatenate([-sin, zeros(HEAD_DIM - ROT_HALF)], axis=1)
    s2 = jnp.concatenate([zeros(ROT_HALF), sin, zeros(HEAD_DIM - ROT_DIM)], axis=1)
    return c, s1, s2


def _window_rows(k, v, g, rows):
    sl = slice(g * SLOT_WIDTH, (g + 1) * SLOT_WIDTH)
    kv = jnp.stack([k[k.shape[0] - rows:, sl], v[v.shape[0] - rows:, sl]], axis=1)
    return kv.reshape(rows, 2, HEADS_PER_GROUP, HEAD_DIM)


def kernel(x_prompt, x_sample, cache_win1_kv, cache_win2_kv, cache_win3_kv, state_conv, state_ffn_conv,
           cache_mem_kv, mem_prompt, w_in, b_gate, w_attn_out, conv_w, conv_b, conv_ln_g, conv_ln_b,
           w_conv_out, w_mix_out, ln1_g, ln1_b, w_xq, w_xkv, w_xo, ln2_g, ln2_b, w_up, ffn_conv_w,
           ffn_conv_b, w_down, ln3_g, ln3_b):
    bp, sp, _ = x_prompt.shape
    nb, ts, _ = x_sample.shape
    assert bp == 1 and ts == 1
    depth = w_in.shape[0]
    assert depth == DEPTH
    win_caches = (cache_win1_kv, cache_win2_kv, cache_win3_kv)

    tables_p = _rope_tables(jnp.arange(sp))
    tables_s = _rope_tables(jnp.full((nb,), PAST_LEN, jnp.int32))
    mem_b = mem_prompt.reshape(N_MEM, D_MODEL).astype(BF16)

    bf = lambda a: a.astype(BF16)
    w_gates = bf(w_in[:, :, OFF_GATES:])
    w_attn_out, w_conv_out = bf(w_attn_out), bf(w_conv_out)
    w_xq, w_xkv, w_up, w_down = bf(w_xq), bf(w_xkv), bf(w_up), bf(w_down)

    xp = x_prompt.reshape(sp, D_MODEL)
    xs = x_sample.reshape(nb, D_MODEL)
    xpb, xsb = bf(xp), bf(xs)
    tm_s = nb
    q_scale = HEAD_DIM ** -0.5
    xq_scale = X_HEAD_DIM ** -0.5

    win_p = [[] for _ in DIL_GROUPS]
    win_s = [[] for _ in DIL_GROUPS]
    conv_p, ffn_p, memkv_p, conv_s, ffn_s = [], [], [], [], []
    for l in range(depth):
        bg = b_gate[l].reshape(1, 2 * D_MODEL)

        mkv, mkv_b = _matmul(mem_b, w_xkv, l, [F32, BF16], N_MEM, "mem_kv")
        qs = _qkv_proj(xpb, w_in, l, OFF_Q, tables_p, True, q_scale, 512, True, [], "q_proj_p")
        *ks, k = _qkv_proj(xpb, w_in, l, OFF_K, tables_p, True, 1.0, 512, True, [F32], "k_proj_p")
        *vs, v = _qkv_proj(xpb, w_in, l, OFF_V, tables_p, False, 1.0, 512, True, [F32], "v_proj_p")
        u = _glu_proj(xpb, w_in, l, 256, "glu_proj_p")
        gates = _gate_proj(xpb, w_gates, l, bg, 2048, "gate_proj_p")
        parts = []
        for g in range(N_GROUPS):
            parts.extend(_win_attn_prompt(qs[g], ks[g], vs[g], g))
        attn = _combine_groups(parts, 512)
        cact = _conv_prompt(u, conv_w[l], conv_b[l], conv_ln_g[l], conv_ln_b[l])
        mix = _mix(attn, cact, gates, w_attn_out, w_conv_out, l, 1024)
        xp, xpb = _res_ln(mix, w_mix_out, l, xp, ln1_g[l], ln1_b[l], 256, "mix_out_ln1")
        (xq,) = _matmul(xpb, w_xq, l, [BF16], 2048, "xq_p", scale=xq_scale)
        xo = _xattn_prompt(xq, mkv_b, 512)
        xp, xpb = _res_ln(xo, w_xo, l, xp, ln2_g[l], ln2_b[l], 256, "xo_ln2")
        f, h_last = _ffn_prompt(xpb, w_up, l, ffn_conv_w[l], ffn_conv_b[l], 1024)
        xp, xpb = _res_ln(f, w_down, l, xp, ln3_g[l], ln3_b[l], 256, "down_ln3")
        for g, (win, _) in enumerate(DIL_GROUPS):
            win_p[g].append(_window_rows(k, v, g, min(win, sp))[None])
        conv_p.append(u[None, sp - (CONV_K - 1):])
        ffn_p.append(h_last[None])
        memkv_p.append(mkv.reshape(1, N_MEM, 2, X_HEADS, X_HEAD_DIM))

        (q,) = _qkv_proj(xsb, w_in, l, OFF_Q, tables_s, True, q_scale, tm_s, False, [F32], "q_proj_s")
        (k,) = _qkv_proj(xsb, w_in, l, OFF_K, tables_s, True, 1.0, tm_s, False, [F32], "k_proj_s")
        (v,) = _qkv_proj(xsb, w_in, l, OFF_V, tables_s, False, 1.0, tm_s, False, [F32], "v_proj_s")
        u = _glu_proj(xsb, w_in, l, tm_s, "glu_proj_s")
        gates = _gate_proj(xsb, w_gates, l, bg, tm_s, "gate_proj_s")
        attn = bf(_win_attn_sample(q, k, v, win_caches, l))
        cact, new_conv = _conv_sample(state_conv, u, conv_w[l], conv_b[l], conv_ln_g[l], conv_ln_b[l], l)
        mix = _mix(attn, cact, gates, w_attn_out, w_conv_out, l, tm_s)
        xs, xsb = _res_ln(mix, w_mix_out, l, xs, ln1_g[l], ln1_b[l], tm_s, "mix_out_ln1")
        (xq,) = _matmul(xsb, w_xq, l, [F32], tm_s, "xq_s", scale=xq_scale)
        xo = bf(_xattn_sample(xq, cache_mem_kv, l))
        xs, xsb = _res_ln(xo, w_xo, l, xs, ln2_g[l], ln2_b[l], tm_s, "xo_ln2")
        f, h_new = _ffn_sample(xsb, w_up, state_ffn_conv, ffn_conv_w[l], ffn_conv_b[l], l)
        xs, xsb = _res_ln(f, w_down, l, xs, ln3_g[l], ln3_b[l], tm_s, "down_ln3")
        for g in range(N_GROUPS):
            win_s[g].append(_window_rows(k, v, g, nb).reshape(nb, 1, 2, HEADS_PER_GROUP, HEAD_DIM))
        conv_s.append(new_conv)
        ffn_s.append(jnp.concatenate([state_ffn_conv[l][:, 1:], h_new[:, None]], axis=1))

    return (xp.reshape(bp, sp, D_MODEL), xs.reshape(nb, ts, D_MODEL),
            jnp.stack(win_p[0]), jnp.stack(win_p[1]), jnp.stack(win_p[2]), jnp.stack(conv_p),
            jnp.stack(ffn_p), jnp.stack(memkv_p),
            jnp.stack(win_s[0]), jnp.stack(win_s[1]), jnp.stack(win_s[2]), jnp.stack(conv_s),
            jnp.stack(ffn_s))
```

```python
import functools

import jax
import jax.numpy as jnp
from jax import lax
from jax.experimental import pallas as pl
from jax.experimental.pallas import tpu as pltpu

F32 = jnp.float32
BF16 = jnp.bfloat16

D_MODEL = 2048
PAST_LEN = 2048
HEAD_DIM = 128
HEADS_PER_GROUP = 4
DIL_GROUPS = ((128, 1), (512, 4), (2048, 16))
DILATIONS = tuple(d for _, d in DIL_GROUPS)
N_GROUPS = len(DIL_GROUPS)
ATTN_HEADS = N_GROUPS * HEADS_PER_GROUP
ATTN_WIDTH = ATTN_HEADS * HEAD_DIM
SLOT_WIDTH = HEADS_PER_GROUP * HEAD_DIM
ROT_DIM = HEAD_DIM // 4
ROT_HALF = ROT_DIM // 2
ROPE_THETA = 500000.0
CONV_WIDTH = 3 * D_MODEL // 4
CONV_K = 31
N_MEM = 256
X_HEADS = 4
X_HEAD_DIM = D_MODEL // X_HEADS
D_FF = 11 * D_MODEL // 4
FFN_K = 3
LN_EPS = 1e-5
DEPTH = 2
DN_ALPHA = (2 * DEPTH) ** 0.25

OFF_Q = 0
OFF_K = ATTN_WIDTH
OFF_V = 2 * ATTN_WIDTH
OFF_A = 3 * ATTN_WIDTH
OFF_BG = OFF_A + CONV_WIDTH
OFF_GATES = OFF_BG + CONV_WIDTH

LANES = 128
SUBLANES = 8
MXU_COLS = 256
KEYS_PER_GROUP = 128
VMEM_CAP = 56 * 1024 * 1024
MASK_VALUE = -1e30
TN = 512
CAST_ROWS = 256


def _params(n_axes, vmem_bytes):
    return pltpu.CompilerParams(dimension_semantics=("arbitrary",) * n_axes,
                                vmem_limit_bytes=int(min(vmem_bytes, VMEM_CAP)))


def _layer_norm(x, g, b):
    mu = jnp.mean(x, axis=-1, keepdims=True)
    xc = x - mu
    var = jnp.mean(xc * xc, axis=-1, keepdims=True)
    return xc * lax.rsqrt(var + LN_EPS) * g + b


def _dot(a, b):
    return jnp.dot(a, b, preferred_element_type=F32)


def _dot_nt(a, b):
    return lax.dot_general(a, b, (((1,), (1,)), ((), ())), preferred_element_type=F32)


def _w_spec(k, layer, off=0):
    return pl.BlockSpec((None, k, TN), lambda i, j: (layer, 0, j + off))


def _resident_w_spec(k, n, layer, col_block):
    return pl.BlockSpec((None, k, n), lambda i: (layer, 0, col_block), pipeline_mode=pl.Buffered(1))


def _cast_weight_once(w_ref, wb_ref):
    @pl.when(pl.program_id(0) == 0)
    def _():
        def rows(i, carry):
            r = pl.ds(pl.multiple_of(i * CAST_ROWS, CAST_ROWS), CAST_ROWS)
            wb_ref[r, :] = w_ref[r, :].astype(wb_ref.dtype)
            return carry
        lax.fori_loop(0, w_ref.shape[0] // CAST_ROWS, rows, 0)


def _rope_head(xh, c, s1, s2, scale):
    r = xh * c + pltpu.roll(xh, HEAD_DIM - ROT_HALF, 1) * s1 + pltpu.roll(xh, ROT_HALF, 1) * s2
    return r * scale if scale != 1.0 else r


def _qkv_kernel(x_ref, w_ref, c_ref, s1_ref, s2_ref, *refs, rope, scale, n_perm, n_plain):
    perm_refs = refs[:n_perm]
    plain_refs = refs[n_perm:n_perm + n_plain]
    wb_ref, scr = refs[n_perm + n_plain:]
    _cast_weight_once(w_ref, wb_ref)
    x = x_ref[...]
    tm = x.shape[0]
    heads_per_chunk = MXU_COLS // HEAD_DIM
    if rope:
        c, s1, s2 = c_ref[...], s1_ref[...], s2_ref[...]
    for chunk in range(ATTN_WIDTH // MXU_COLS):
        acc = _dot(x, wb_ref[:, chunk * MXU_COLS:(chunk + 1) * MXU_COLS])
        for hc in range(heads_per_chunk):
            head = chunk * heads_per_chunk + hc
            xh = acc[:, hc * HEAD_DIM:(hc + 1) * HEAD_DIM]
            if rope:
                xh = _rope_head(xh, c, s1, s2, scale)
            sl = slice(head * HEAD_DIM, (head + 1) * HEAD_DIM)
            for o in plain_refs:
                o[:, sl] = xh.astype(o.dtype)
            if n_perm:
                scr[head] = xh
    for g in range(n_perm):
        dil = DILATIONS[g]
        o = perm_refs[g]
        for h in range(HEADS_PER_GROUP):
            head = g * HEADS_PER_GROUP + h
            sl = slice(h * HEAD_DIM, (h + 1) * HEAD_DIM)
            for r in range(dil):
                rows = scr[head] if dil == 1 else scr[head, pl.ds(r, tm // dil, stride=dil), :]
                o[r, :, sl] = rows.astype(o.dtype)


def _qkv_proj(x, w_in, layer, col_off, tables, rope, scale, tm, perm, plain_dtypes, name, plain_rows=None):
    m, k = x.shape
    n_perm = N_GROUPS if perm else 0
    tab_spec = pl.BlockSpec((tm, HEAD_DIM), lambda i: (i, 0))
    perm_shapes = [jax.ShapeDtypeStruct((d, m // d, SLOT_WIDTH), BF16) for d in DILATIONS[:n_perm]]
    perm_specs = [pl.BlockSpec((d, tm // d, SLOT_WIDTH), lambda i: (0, i, 0)) for d in DILATIONS[:n_perm]]
    plain_rows = plain_rows or m
    skipped = (m - plain_rows) // tm
    plain_shapes = [jax.ShapeDtypeStruct((plain_rows, ATTN_WIDTH), d) for d in plain_dtypes]
    plain_specs = [pl.BlockSpec((tm, ATTN_WIDTH), lambda i: (jnp.maximum(i - skipped, 0), 0))
                   for _ in plain_dtypes]
    vmem = (2 * tm * k * 2 + k * ATTN_WIDTH * 6 + 2 * 3 * tm * HEAD_DIM * 4 + tm * ATTN_WIDTH * 4
            + 2 * n_perm * tm * SLOT_WIDTH * 2
            + 2 * sum(tm * ATTN_WIDTH * jnp.dtype(d).itemsize for d in plain_dtypes)
            + tm * MXU_COLS * 32 + (6 << 20))
    return pl.pallas_call(
        functools.partial(_qkv_kernel, rope=rope, scale=scale, n_perm=n_perm, n_plain=len(plain_dtypes)),
        out_shape=perm_shapes + plain_shapes,
        grid=(m // tm,),
        in_specs=[pl.BlockSpec((tm, k), lambda i: (i, 0)),
                  _resident_w_spec(k, ATTN_WIDTH, layer, col_off // ATTN_WIDTH),
                  tab_spec, tab_spec, tab_spec],
        out_specs=perm_specs + plain_specs,
        scratch_shapes=[pltpu.VMEM((k, ATTN_WIDTH), BF16), pltpu.VMEM((ATTN_HEADS, tm, HEAD_DIM), F32)],
        compiler_params=_params(1, vmem),
        name=name,
    )(x, w_in, *tables)


def _glu_kernel(x_ref, wa_ref, wb_ref, o_ref, wa_scr, wb_scr):
    _cast_weight_once(wa_ref, wa_scr)
    _cast_weight_once(wb_ref, wb_scr)
    x = x_ref[...]
    for chunk in range(CONV_WIDTH // MXU_COLS):
        cs = slice(chunk * MXU_COLS, (chunk + 1) * MXU_COLS)
        o_ref[:, cs] = _dot(x, wa_scr[:, cs]) * jax.nn.sigmoid(_dot(x, wb_scr[:, cs]))


def _glu_proj(x, w_in, layer, tm, name):
    m, k = x.shape
    vmem = 2 * tm * k * 2 + 2 * k * CONV_WIDTH * 6 + 2 * tm * CONV_WIDTH * 4 + tm * MXU_COLS * 32 + (4 << 20)
    return pl.pallas_call(
        _glu_kernel,
        out_shape=jax.ShapeDtypeStruct((m, CONV_WIDTH), F32),
        grid=(m // tm,),
        in_specs=[pl.BlockSpec((tm, k), lambda i: (i, 0)),
                  _resident_w_spec(k, CONV_WIDTH, layer, OFF_A // CONV_WIDTH),
                  _resident_w_spec(k, CONV_WIDTH, layer, OFF_BG // CONV_WIDTH)],
        out_specs=pl.BlockSpec((tm, CONV_WIDTH), lambda i: (i, 0)),
        scratch_shapes=[pltpu.VMEM((k, CONV_WIDTH), BF16)] * 2,
        compiler_params=_params(1, vmem),
        name=name,
    )(x, w_in, w_in)


def _gate_kernel(x_ref, w_ref, b_ref, o_ref):
    x = x_ref[...]
    for chunk in range(TN // MXU_COLS):
        cs = slice(chunk * MXU_COLS, (chunk + 1) * MXU_COLS)
        o_ref[:, cs] = jax.nn.sigmoid(_dot(x, w_ref[:, cs]) + b_ref[:, cs]).astype(o_ref.dtype)


def _gate_proj(x, w_gates, layer, b_gate, tm, name):
    m, k = x.shape
    vmem = 2 * (tm * k * 2 + k * TN * 2 + tm * TN * 2) + tm * TN * 16
    return pl.pallas_call(
        _gate_kernel,
        out_shape=jax.ShapeDtypeStruct((m, 2 * D_MODEL), BF16),
        grid=(m // tm, 2 * D_MODEL // TN),
        in_specs=[pl.BlockSpec((tm, k), lambda i, j: (i, 0)), _w_spec(k, layer),
                  pl.BlockSpec((1, TN), lambda i, j: (0, j))],
        out_specs=pl.BlockSpec((tm, TN), lambda i, j: (i, j)),
        compiler_params=_params(2, vmem + (8 << 20)),
        name=name,
    )(x, w_gates, b_gate)


def _mm_kernel(x_ref, w_ref, *out_refs, scale):
    acc = _dot(x_ref[...], w_ref[...].astype(BF16))
    if scale != 1.0:
        acc = acc * scale
    for o in out_refs:
        o[...] = acc.astype(o.dtype)


def _matmul(x, w, layer, out_dtypes, tm, name, scale=1.0):
    m, k = x.shape
    n_cols = w.shape[2]
    vmem = 2 * (tm * k * 2 + k * TN * jnp.dtype(w.dtype).itemsize
                + sum(tm * TN * jnp.dtype(d).itemsize for d in out_dtypes)) + tm * TN * 8 + k * TN * 2
    return pl.pallas_call(
        functools.partial(_mm_kernel, scale=scale),
        out_shape=[jax.ShapeDtypeStruct((m, n_cols), d) for d in out_dtypes],
        grid=(m // tm, n_cols // TN),
        in_specs=[pl.BlockSpec((tm, k), lambda i, j: (i, 0)), _w_spec(k, layer)],
        out_specs=[pl.BlockSpec((tm, TN), lambda i, j: (i, j)) for _ in out_dtypes],
        compiler_params=_params(2, vmem + (8 << 20)),
        name=name,
    )(x, w)


WIN_BLOCKS_PER_STEP = 4


def _win_attn_kernel(q_ref, kc_ref, kp_ref, vc_ref, vp_ref, o_ref, l_ref):
    i = pl.program_id(1)
    tq = KEYS_PER_GROUP
    row = lax.broadcasted_iota(jnp.int32, (tq, tq), 0)
    col = lax.broadcasted_iota(jnp.int32, (tq, tq), 1)
    mask_c = col <= row
    mask_p = col >= row
    mask_first = jnp.logical_and(mask_p, i > 0)
    for blk in range(q_ref.shape[0] // tq):
        rows = slice(blk * tq, (blk + 1) * tq)
        for h in range(HEADS_PER_GROUP):
            sl = slice(h * HEAD_DIM, (h + 1) * HEAD_DIM)
            q = q_ref[rows, sl]
            if blk == 0:
                k_prev, v_prev, mask_prev = kp_ref[:, sl], vp_ref[:, sl], mask_first
            else:
                before = slice((blk - 1) * tq, blk * tq)
                k_prev, v_prev, mask_prev = kc_ref[before, sl], vc_ref[before, sl], mask_p
            sc = jnp.where(mask_c, _dot_nt(q, kc_ref[rows, sl]), MASK_VALUE)
            sp = jnp.where(mask_prev, _dot_nt(q, k_prev), MASK_VALUE)
            m = jnp.maximum(jnp.max(sc, axis=-1, keepdims=True), jnp.max(sp, axis=-1, keepdims=True))
            pc = jnp.exp(sc - m)
            pp = jnp.exp(sp - m)
            den = jnp.sum(pc, axis=-1, keepdims=True) + jnp.sum(pp, axis=-1, keepdims=True)
            o = _dot(pc.astype(BF16), vc_ref[rows, sl]) + _dot(pp.astype(BF16), v_prev)
            o_ref[rows, sl] = o * (1.0 / den)
            l_ref[rows, sl] = jnp.broadcast_to(m + jnp.log(den), (tq, HEAD_DIM))


def _win_attn_prompt(q, k, v, g):
    dil, tc, _ = q.shape
    tq = WIN_BLOCKS_PER_STEP * KEYS_PER_GROUP
    cur = pl.BlockSpec((None, tq, SLOT_WIDTH), lambda r, i: (r, i, 0))
    prev = pl.BlockSpec((None, KEYS_PER_GROUP, SLOT_WIDTH),
                        lambda r, i: (r, jnp.maximum(i * WIN_BLOCKS_PER_STEP - 1, 0), 0))
    return pl.pallas_call(
        _win_attn_kernel,
        out_shape=[jax.ShapeDtypeStruct((dil, tc, SLOT_WIDTH), F32)] * 2,
        grid=(dil, tc // tq),
        in_specs=[cur, cur, prev, cur, prev],
        out_specs=[cur, cur],
        compiler_params=_params(2, 24 << 20),
        name=f"win_attn_g{g}",
    )(q, k, k, v, v)


def _combine_kernel(o1, l1, o2, l2, o3, l3, out_ref, so2, sl2, so3, sl3):
    tm = out_ref.shape[0]
    lane_blocks = SLOT_WIDTH // LANES
    for src, dst, dil in ((o2, so2, DILATIONS[1]), (l2, sl2, DILATIONS[1]),
                          (o3, so3, DILATIONS[2]), (l3, sl3, DILATIONS[2])):
        for c in range(lane_blocks):
            for r in range(dil):
                dst[c, pl.ds(r, tm // dil, stride=dil), :] = src[r, :, c * LANES:(c + 1) * LANES]
    for c in range(lane_blocks):
        ls = slice(c * LANES, (c + 1) * LANES)
        la, lb, lc = l1[0, :, ls], sl2[c], sl3[c]
        m = jnp.maximum(jnp.maximum(la, lb), lc)
        ea, eb, ec = jnp.exp(la - m), jnp.exp(lb - m), jnp.exp(lc - m)
        num = ea * o1[0, :, ls] + eb * so2[c] + ec * so3[c]
        out_ref[:, ls] = (num * (1.0 / (ea + eb + ec))).astype(out_ref.dtype)


def _combine_groups(parts, tm):
    t = parts[0].shape[1]
    specs = []
    for dil in DILATIONS:
        specs += [pl.BlockSpec((dil, tm // dil, SLOT_WIDTH), lambda i: (0, i, 0))] * 2
    return pl.pallas_call(
        _combine_kernel,
        out_shape=jax.ShapeDtypeStruct((t, SLOT_WIDTH), BF16),
        grid=(t // tm,),
        in_specs=specs,
        out_specs=pl.BlockSpec((tm, SLOT_WIDTH), lambda i: (i, 0)),
        scratch_shapes=[pltpu.VMEM((SLOT_WIDTH // LANES, tm, LANES), F32)] * 4,
        compiler_params=_params(1, 32 << 20),
        name="combine_groups",
    )(*parts)


KV_ROWS = 2 * HEADS_PER_GROUP


def _win_decode_kernel(q_ref, kvn_ref, c1_ref, c2_ref, c3_ref, o_ref):
    caches = (c1_ref, c2_ref, c3_ref)
    bs = q_ref.shape[0]

    def body(b, carry):
        q8 = q_ref[b]
        scores = [jnp.sum(kvn_ref[b] * q8, axis=-1, keepdims=True)]
        for g in range(N_GROUPS):
            scores.append(jnp.sum(caches[g][b] * q8[g][None], axis=-1, keepdims=True))
        m = jnp.max(scores[0], axis=0)
        for s in scores[1:]:
            m = jnp.maximum(m, jnp.max(s, axis=0))
        num = jnp.zeros((KV_ROWS, HEAD_DIM), F32)
        den = jnp.zeros((KV_ROWS, 1), F32)
        for s, rows in zip(scores, [kvn_ref[b]] + [c[b] for c in caches]):
            p = jnp.exp(s - m[None])
            den = den + jnp.sum(p, axis=0)
            p_val = pltpu.roll(jnp.broadcast_to(p, rows.shape), HEADS_PER_GROUP, 1)
            num = num + jnp.sum(p_val * rows, axis=0)
        o_ref[b] = num * (1.0 / pltpu.roll(den, HEADS_PER_GROUP, 0))
        return carry

    lax.fori_loop(0, bs, body, 0, unroll=2)


def _win_attn_sample(q, k_new, v_new, caches, layer):
    b = q.shape[0]
    bs = 4
    views, specs = [], []
    for cache, (win, dil) in zip(caches, DIL_GROUPS):
        depth, _, n_past = cache.shape[:3]
        assert n_past == win and n_past % dil == 0
        views.append(cache.reshape(depth, b, n_past // dil, dil, KV_ROWS, HEAD_DIM))
        specs.append(pl.BlockSpec((None, bs, KEYS_PER_GROUP, None, KV_ROWS, HEAD_DIM),
                                  lambda i: (layer, i, 0, 0, 0, 0)))
    heads = lambda a: a.reshape(b, N_GROUPS, HEADS_PER_GROUP, HEAD_DIM)
    q8 = jnp.concatenate([heads(q), heads(q)], axis=2)
    kv8 = jnp.concatenate([heads(k_new), heads(v_new)], axis=2)
    row_spec = pl.BlockSpec((bs, N_GROUPS, KV_ROWS, HEAD_DIM), lambda i: (i, 0, 0, 0))
    vmem = 2 * (3 * bs * KEYS_PER_GROUP * KV_ROWS * HEAD_DIM * 4) + (16 << 20)
    out = pl.pallas_call(
        _win_decode_kernel,
        out_shape=jax.ShapeDtypeStruct((b, KV_ROWS, HEAD_DIM), F32),
        grid=(b // bs,),
        in_specs=[row_spec, row_spec] + specs,
        out_specs=pl.BlockSpec((bs, KV_ROWS, HEAD_DIM), lambda i: (i, 0, 0)),
        compiler_params=_params(1, vmem),
        name="win_attn_sample",
    )(q8, kv8, *views)
    return out[:, HEADS_PER_GROUP:].reshape(b, SLOT_WIDTH)


CONV_HALO = 32
CONV_ROWS = 64
LN_ROWS = 16


def _conv_prompt_kernel(u_ref, halo_ref, w_ref, b_ref, g_ref, beta_ref, o_ref, ext_ref, sh_ref, c_ref):
    i = pl.program_id(0)
    tt = u_ref.shape[0]
    first_tap = CONV_HALO - (CONV_K - 1)
    halo = halo_ref[...]
    ext_ref[0:CONV_HALO, :] = jnp.where(i > 0, halo, jnp.zeros_like(halo))
    ext_ref[CONV_HALO:, :] = u_ref[...]
    n_sh = tt + CONV_HALO - SUBLANES
    for s in range(1, SUBLANES):
        sh_ref[s - 1] = ext_ref[pl.ds(s, n_sh), :]

    n_tiles = CONV_ROWS // SUBLANES

    def tap_rows(ci, carry):
        r0 = pl.multiple_of(ci * CONV_ROWS, CONV_ROWS)
        for lb in range(CONV_WIDTH // LANES):
            ls = slice(lb * LANES, (lb + 1) * LANES)
            bias = jnp.broadcast_to(b_ref[:, ls], (SUBLANES, LANES))
            acc = [bias] * n_tiles
            for s in range(SUBLANES):
                taps = [(a, a * SUBLANES + s - first_tap) for a in range((CONV_HALO + SUBLANES) // SUBLANES)
                        if 0 <= a * SUBLANES + s - first_tap < CONV_K]
                a_lo, a_hi = taps[0][0], taps[-1][0]
                rows = pl.ds(r0 + a_lo * SUBLANES, CONV_ROWS + (a_hi - a_lo) * SUBLANES)
                strip = ext_ref[rows, ls] if s == 0 else sh_ref[s - 1, rows, ls]
                for a, j in taps:
                    wj = w_ref[j:j + 1, ls]
                    for k in range(n_tiles):
                        t0 = (k + a - a_lo) * SUBLANES
                        acc[k] = acc[k] + strip[t0:t0 + SUBLANES] * wj
            c_ref[pl.ds(r0, CONV_ROWS), ls] = jnp.concatenate(acc, axis=0)
        return carry

    lax.fori_loop(0, tt // CONV_ROWS, tap_rows, 0)

    def ln_rows(ci, carry):
        r0 = pl.multiple_of(ci * LN_ROWS, LN_ROWS)
        y = _layer_norm(c_ref[pl.ds(r0, LN_ROWS), :], g_ref[...], beta_ref[...])
        o_ref[pl.ds(r0, LN_ROWS), :] = (y * jax.nn.sigmoid(y)).astype(o_ref.dtype)
        return carry

    lax.fori_loop(0, tt // LN_ROWS, ln_rows, 0, unroll=4)


def _conv_prompt(u, conv_w, conv_b, ln_g, ln_b):
    t = u.shape[0]
    tt = 256
    row = lambda a: a.reshape(1, CONV_WIDTH)
    full = lambda shape: pl.BlockSpec(shape, lambda i: (0, 0))
    hb = tt // CONV_HALO
    vmem = (2 * (tt + CONV_HALO) * CONV_WIDTH * 4 + 2 * tt * CONV_WIDTH * 2
            + (8 * (tt + CONV_HALO) + tt) * CONV_WIDTH * 4 + (8 << 20))
    return pl.pallas_call(
        _conv_prompt_kernel,
        out_shape=jax.ShapeDtypeStruct((t, CONV_WIDTH), BF16),
        grid=(t // tt,),
        in_specs=[pl.BlockSpec((tt, CONV_WIDTH), lambda i: (i, 0)),
                  pl.BlockSpec((CONV_HALO, CONV_WIDTH), lambda i: (jnp.maximum(i * hb - 1, 0), 0)),
                  full((CONV_K, CONV_WIDTH)), full((1, CONV_WIDTH)),
                  full((1, CONV_WIDTH)), full((1, CONV_WIDTH))],
        out_specs=pl.BlockSpec((tt, CONV_WIDTH), lambda i: (i, 0)),
        scratch_shapes=[pltpu.VMEM((tt + CONV_HALO, CONV_WIDTH), F32),
                        pltpu.VMEM((SUBLANES - 1, tt + CONV_HALO - SUBLANES, CONV_WIDTH), F32),
                        pltpu.VMEM((tt, CONV_WIDTH), F32)],
        compiler_params=_params(1, vmem),
        name="conv_prompt",
    )(u, u, conv_w, row(conv_b), row(ln_g), row(ln_b))


def _conv_sample_kernel(st_ref, u_ref, w_ref, b_ref, g_ref, beta_ref, o_ref, ns_ref, c_ref):
    bs = u_ref.shape[0]
    hist = CONV_K - 1
    w_hist = w_ref[0:hist, :]
    for b in range(bs):
        st = st_ref[b]
        c_ref[b:b + 1, :] = jnp.sum(st * w_hist, axis=0, keepdims=True)
        ns_ref[b, 0:hist - 1, :] = st[1:hist]
        ns_ref[b, hist - 1:hist, :] = u_ref[b:b + 1, :]
    acc = c_ref[...] + u_ref[...] * w_ref[CONV_K - 1:CONV_K, :] + b_ref[...]
    y = _layer_norm(acc, g_ref[...], beta_ref[...])
    o_ref[...] = (y * jax.nn.sigmoid(y)).astype(o_ref.dtype)


def _conv_sample(state, u, conv_w, conv_b, ln_g, ln_b, layer):
    b = u.shape[0]
    bs = 16
    row = lambda a: a.reshape(1, CONV_WIDTH)
    full = lambda shape: pl.BlockSpec(shape, lambda i: (0, 0))
    return pl.pallas_call(
        _conv_sample_kernel,
        out_shape=[jax.ShapeDtypeStruct((b, CONV_WIDTH), BF16),
                   jax.ShapeDtypeStruct((b, CONV_K - 1, CONV_WIDTH), F32)],
        grid=(b // bs,),
        in_specs=[pl.BlockSpec((None, bs, CONV_K - 1, CONV_WIDTH), lambda i: (layer, i, 0, 0)),
                  pl.BlockSpec((bs, CONV_WIDTH), lambda i: (i, 0)),
                  full((CONV_K, CONV_WIDTH)), full((1, CONV_WIDTH)),
                  full((1, CONV_WIDTH)), full((1, CONV_WIDTH))],
        out_specs=[pl.BlockSpec((bs, CONV_WIDTH), lambda i: (i, 0)),
                   pl.BlockSpec((bs, CONV_K - 1, CONV_WIDTH), lambda i: (i, 0, 0))],
        scratch_shapes=[pltpu.VMEM((bs, CONV_WIDTH), F32)],
        compiler_params=_params(1, 4 * bs * 32 * CONV_WIDTH * 4 + (8 << 20)),
        name="conv_sample",
    )(state, u, conv_w, row(conv_b), row(ln_g), row(ln_b))


def _mix_kernel(a_ref, c_ref, wa_ref, wc_ref, ga_ref, gc_ref, o_ref):
    a = _dot(a_ref[...], wa_ref[...])
    c = _dot(c_ref[...], wc_ref[...])
    o_ref[...] = (ga_ref[...].astype(F32) * a + gc_ref[...].astype(F32) * c).astype(o_ref.dtype)


def _mix(attn, cact, gates, w_attn_out, w_conv_out, layer, tm):
    m = attn.shape[0]
    ng = D_MODEL // TN
    tile = pl.BlockSpec((tm, TN), lambda i, j: (i, j))
    vmem = 2 * (tm * (SLOT_WIDTH + CONV_WIDTH) * 2 + (SLOT_WIDTH + CONV_WIDTH) * TN * 2
                + 3 * tm * TN * 2) + tm * TN * 16 + (8 << 20)
    return pl.pallas_call(
        _mix_kernel,
        out_shape=jax.ShapeDtypeStruct((m, D_MODEL), BF16),
        grid=(m // tm, D_MODEL // TN),
        in_specs=[pl.BlockSpec((tm, SLOT_WIDTH), lambda i, j: (i, 0)),
                  pl.BlockSpec((tm, CONV_WIDTH), lambda i, j: (i, 0)),
                  _w_spec(SLOT_WIDTH, layer), _w_spec(CONV_WIDTH, layer), tile,
                  pl.BlockSpec((tm, TN), lambda i, j: (i, j + ng))],
        out_specs=tile,
        compiler_params=_params(2, vmem),
        name="mix",
    )(attn, cact, w_attn_out, w_conv_out, gates, gates)


def _res_ln_kernel(h_ref, w_ref, x_ref, g_ref, b_ref, o_ref, ob_ref, *scratch):
    if scratch:
        _cast_weight_once(w_ref, scratch[0])
        w = scratch[0][...]
    else:
        w = w_ref[...]
    y = DN_ALPHA * x_ref[...] + _dot(h_ref[...], w)
    y = _layer_norm(y, g_ref[...], b_ref[...])
    o_ref[...] = y
    ob_ref[...] = y.astype(ob_ref.dtype)


def _res_ln(h, w, layer, x, g, b, tm, name):
    m, k = h.shape
    cast = w.dtype != BF16
    full = lambda shape: pl.BlockSpec(shape, lambda i: (0, 0))
    w_bytes = k * D_MODEL * (6 if cast else 2)
    vmem = 2 * (tm * k * 2 + tm * D_MODEL * (4 + 4 + 2)) + w_bytes + tm * D_MODEL * 12 + (4 << 20)
    return pl.pallas_call(
        _res_ln_kernel,
        out_shape=[jax.ShapeDtypeStruct((m, D_MODEL), F32), jax.ShapeDtypeStruct((m, D_MODEL), BF16)],
        grid=(m // tm,),
        in_specs=[pl.BlockSpec((tm, k), lambda i: (i, 0)),
                  _resident_w_spec(k, D_MODEL, layer, 0),
                  pl.BlockSpec((tm, D_MODEL), lambda i: (i, 0)),
                  full((1, D_MODEL)), full((1, D_MODEL))],
        out_specs=[pl.BlockSpec((tm, D_MODEL), lambda i: (i, 0))] * 2,
        scratch_shapes=[pltpu.VMEM((k, D_MODEL), BF16)] if cast else [],
        compiler_params=_params(1, vmem),
        name=name,
    )(h, w, x, g.reshape(1, D_MODEL), b.reshape(1, D_MODEL))


def _xattn_prompt_kernel(q_ref, kv_ref, o_ref):
    for h in range(X_HEADS):
        sl = slice(h * X_HEAD_DIM, (h + 1) * X_HEAD_DIM)
        s = _dot_nt(q_ref[:, sl], kv_ref[:, sl])
        p = jnp.exp(s - jnp.max(s, axis=-1, keepdims=True))
        den = jnp.sum(p, axis=-1, keepdims=True)
        o = _dot(p.astype(BF16), kv_ref[:, D_MODEL + h * X_HEAD_DIM:D_MODEL + (h + 1) * X_HEAD_DIM])
        o_ref[:, sl] = (o * (1.0 / den)).astype(o_ref.dtype)


def _xattn_prompt(q, mem_kv, tm):
    t = q.shape[0]
    return pl.pallas_call(
        _xattn_prompt_kernel,
        out_shape=jax.ShapeDtypeStruct((t, D_MODEL), BF16),
        grid=(t // tm,),
        in_specs=[pl.BlockSpec((tm, D_MODEL), lambda i: (i, 0)),
                  pl.BlockSpec((N_MEM, 2 * D_MODEL), lambda i: (0, 0))],
        out_specs=pl.BlockSpec((tm, D_MODEL), lambda i: (i, 0)),
        compiler_params=_params(1, 32 << 20),
        name="xattn_prompt",
    )(q, mem_kv)


XATTN_KEY_CHUNK = 32


def _xattn_sample_kernel(q_ref, kv_ref, o_ref):
    bs = q_ref.shape[0]
    nl = X_HEAD_DIM // LANES
    qs = [q_ref[b] for b in range(bs)]

    def chunk(ci, carry):
        k0 = pl.multiple_of(ci * XATTN_KEY_CHUNK, XATTN_KEY_CHUNK)
        keys = pl.ds(k0, XATTN_KEY_CHUNK)
        out = []
        for b in range(bs):
            m, den, num = carry[b]
            q = qs[b]
            acc = kv_ref[b, keys, 0, :, 0:LANES] * q[None, :, 0:LANES]
            for c in range(1, nl):
                ls = slice(c * LANES, (c + 1) * LANES)
                acc = acc + kv_ref[b, keys, 0, :, ls] * q[None, :, ls]
            s = jnp.sum(acc, axis=-1, keepdims=True)
            m_new = jnp.maximum(m, jnp.max(s, axis=0))
            alpha = jnp.exp(m - m_new)
            p = jnp.exp(s - m_new[None])
            den = den * alpha + jnp.sum(p, axis=0)
            num = num * alpha + jnp.sum(p * kv_ref[b, keys, 1], axis=0)
            out.append((m_new, den, num))
        return tuple(out)

    init = tuple((jnp.full((X_HEADS, 1), MASK_VALUE, F32), jnp.zeros((X_HEADS, 1), F32),
                  jnp.zeros((X_HEADS, X_HEAD_DIM), F32)) for _ in range(bs))
    final = lax.fori_loop(0, N_MEM // XATTN_KEY_CHUNK, chunk, init)
    for b in range(bs):
        _, den, num = final[b]
        o_ref[b] = num * (1.0 / den)


def _xattn_sample(q, cache_mem_kv, layer):
    b = q.shape[0]
    bs = 2
    out = pl.pallas_call(
        _xattn_sample_kernel,
        out_shape=jax.ShapeDtypeStruct((b, X_HEADS, X_HEAD_DIM), F32),
        grid=(b // bs,),
        in_specs=[pl.BlockSpec((bs, X_HEADS, X_HEAD_DIM), lambda i: (i, 0, 0)),
                  pl.BlockSpec((None, bs, N_MEM, 2, X_HEADS, X_HEAD_DIM), lambda i: (layer, i, 0, 0, 0, 0))],
        out_specs=pl.BlockSpec((bs, X_HEADS, X_HEAD_DIM), lambda i: (i, 0, 0)),
        compiler_params=_params(1, 2 * 2 * bs * N_MEM * 2 * D_MODEL * 4 + (8 << 20)),
        name="xattn_sample",
    )(q.reshape(b, X_HEADS, X_HEAD_DIM), cache_mem_kv)
    return out.reshape(b, D_MODEL)


def _ffn_conv(h, tail, cw, cb):
    rows = lax.broadcasted_iota(jnp.int32, tail.shape, 0)
    out = cb + cw[FFN_K - 1:FFN_K] * h
    for back in range(1, FFN_K):
        shifted = pltpu.roll(h, back, 0)
        top = jnp.where(rows < back, pltpu.roll(tail, back, 0), shifted[0:SUBLANES])
        shifted = jnp.concatenate([top, shifted[SUBLANES:]], axis=0)
        out = out + cw[FFN_K - 1 - back:FFN_K - back] * shifted
    return out


def _ffn_prompt_kernel(x_ref, wg_ref, wv_ref, cwg_ref, cwv_ref, cbg_ref, cbv_ref, f_ref, lg_ref, lv_ref,
                       tg_ref, tv_ref):
    m = pl.program_id(1)
    tm = x_ref.shape[0]

    @pl.when(jnp.logical_and(pl.program_id(0) == 0, m == 0))
    def _():
        tg_ref[...] = jnp.zeros(tg_ref.shape, F32)
        tv_ref[...] = jnp.zeros(tv_ref.shape, F32)

    x = x_ref[...]
    tail_g = jnp.where(m > 0, tg_ref[...], 0.0)
    tail_v = jnp.where(m > 0, tv_ref[...], 0.0)
    new_tails, outs = [], []
    for chunk in range(TN // MXU_COLS):
        cs = slice(chunk * MXU_COLS, (chunk + 1) * MXU_COLS)
        hg = _dot(x, wg_ref[:, cs])
        hv = _dot(x, wv_ref[:, cs])
        gate = _ffn_conv(hg, tail_g[:, cs], cwg_ref[:, cs], cbg_ref[:, cs])
        val = _ffn_conv(hv, tail_v[:, cs], cwv_ref[:, cs], cbv_ref[:, cs])
        outs.append((gate * jax.nn.sigmoid(gate) * val).astype(f_ref.dtype))
        new_tails.append((hg[tm - SUBLANES:], hv[tm - SUBLANES:]))
    for chunk, (last_g, last_v) in enumerate(new_tails):
        cs = slice(chunk * MXU_COLS, (chunk + 1) * MXU_COLS)
        f_ref[:, cs] = outs[chunk]
        tg_ref[:, cs] = last_g
        tv_ref[:, cs] = last_v
        lg_ref[:, cs] = last_g
        lv_ref[:, cs] = last_v


def _ffn_prompt(x, w_up, layer, conv_w, conv_b, tm):
    t, k = x.shape
    nt = D_FF // TN
    cb = conv_b.reshape(1, 2 * D_FF)
    vmem = 2 * (tm * k * 2 + 2 * k * TN * 2 + tm * TN * 2) + tm * TN * 48 + (8 << 20)
    f, last_g, last_v = pl.pallas_call(
        _ffn_prompt_kernel,
        out_shape=[jax.ShapeDtypeStruct((t, D_FF), BF16),
                   jax.ShapeDtypeStruct((SUBLANES, D_FF), F32),
                   jax.ShapeDtypeStruct((SUBLANES, D_FF), F32)],
        grid=(nt, t // tm),
        in_specs=[pl.BlockSpec((tm, k), lambda n, m: (m, 0)),
                  pl.BlockSpec((None, k, TN), lambda n, m: (layer, 0, n)),
                  pl.BlockSpec((None, k, TN), lambda n, m: (layer, 0, n + nt)),
                  pl.BlockSpec((FFN_K, TN), lambda n, m: (0, n)),
                  pl.BlockSpec((FFN_K, TN), lambda n, m: (0, n + nt)),
                  pl.BlockSpec((1, TN), lambda n, m: (0, n)),
                  pl.BlockSpec((1, TN), lambda n, m: (0, n + nt))],
        out_specs=[pl.BlockSpec((tm, TN), lambda n, m: (m, n)),
                   pl.BlockSpec((SUBLANES, TN), lambda n, m: (0, n)),
                   pl.BlockSpec((SUBLANES, TN), lambda n, m: (0, n))],
        scratch_shapes=[pltpu.VMEM((SUBLANES, TN), F32)] * 2,
        compiler_params=_params(2, vmem),
        name="ffn_prompt",
    )(x, w_up, w_up, conv_w, conv_w, cb, cb)
    h_last = jnp.concatenate([last_g, last_v], axis=1)[SUBLANES - (FFN_K - 1):]
    return f, h_last


def _ffn_sample_kernel(x_ref, wg_ref, wv_ref, s0g_ref, s0v_ref, s1g_ref, s1v_ref,
                       cwg_ref, cwv_ref, cbg_ref, cbv_ref, f_ref, hg_ref, hv_ref):
    x = x_ref[...]

    def branch(w_ref, s0_ref, s1_ref, cw_ref, cb_ref, h_ref):
        h = _dot(x, w_ref[...])
        h_ref[...] = h
        return cb_ref[...] + cw_ref[0:1, :] * s0_ref[...] + cw_ref[1:2, :] * s1_ref[...] + cw_ref[2:3, :] * h

    gate = branch(wg_ref, s0g_ref, s1g_ref, cwg_ref, cbg_ref, hg_ref)
    val = branch(wv_ref, s0v_ref, s1v_ref, cwv_ref, cbv_ref, hv_ref)
    f_ref[...] = (gate * jax.nn.sigmoid(gate) * val).astype(f_ref.dtype)


def _ffn_sample(x, w_up, state, conv_w, conv_b, layer):
    b, k = x.shape
    nt = D_FF // TN
    st = state.reshape(state.shape[0], b, (FFN_K - 1) * 2 * D_FF)
    cb = conv_b.reshape(1, 2 * D_FF)
    st_spec = lambda off: pl.BlockSpec((None, b, TN), lambda n: (layer, 0, n + off))
    w_spec = lambda off: pl.BlockSpec((None, k, TN), lambda n: (layer, 0, n + off))
    col = lambda rows, off: pl.BlockSpec((rows, TN), lambda n: (0, n + off))
    f, hg, hv = pl.pallas_call(
        _ffn_sample_kernel,
        out_shape=[jax.ShapeDtypeStruct((b, D_FF), BF16),
                   jax.ShapeDtypeStruct((b, D_FF), F32), jax.ShapeDtypeStruct((b, D_FF), F32)],
        grid=(nt,),
        in_specs=[pl.BlockSpec((b, k), lambda n: (0, 0)),
                  w_spec(0), w_spec(nt),
                  st_spec(0), st_spec(nt), st_spec(2 * nt), st_spec(3 * nt),
                  col(FFN_K, 0), col(FFN_K, nt), col(1, 0), col(1, nt)],
        out_specs=[pl.BlockSpec((b, TN), lambda n: (0, n))] * 3,
        compiler_params=_params(1, 32 << 20),
        name="ffn_sample",
    )(x, w_up, w_up, st, st, st, st, conv_w, conv_w, cb, cb)
    return f, jnp.concatenate([hg, hv], axis=1)


def _rope_tables(pos):
    t = pos.shape[0]
    inv = ROPE_THETA ** (-jnp.arange(0, ROT_DIM, 2, dtype=F32) / ROT_DIM)
    ang = pos.astype(F32)[:, None] * inv[None, :]
    cos, sin = jnp.cos(ang), jnp.sin(ang)
    zeros = lambda n: jnp.zeros((t, n), F32)
    c = jnp.concatenate([cos, cos, jnp.ones((t, HEAD_DIM - ROT_DIM), F32)], axis=1)
    s1 = jnp.concatenate([-sin, zeros(HEAD_DIM - ROT_HALF)], axis=1)
    s2 = jnp.concatenate([zeros(ROT_HALF), sin, zeros(HEAD_DIM - ROT_DIM)], axis=1)
    return c, s1, s2


def _window_rows(k, v, g, rows):
    sl = slice(g * SLOT_WIDTH, (g + 1) * SLOT_WIDTH)
    kv = jnp.stack([k[k.shape[0] - rows:, sl], v[v.shape[0] - rows:, sl]], axis=1)
    return kv.reshape(rows, 2, HEADS_PER_GROUP, HEAD_DIM)


def kernel(x_prompt, x_sample, cache_win1_kv, cache_win2_kv, cache_win3_kv, state_conv, state_ffn_conv,
           cache_mem_kv, mem_prompt, w_in, b_gate, w_attn_out, conv_w, conv_b, conv_ln_g, conv_ln_b,
           w_conv_out, w_mix_out, ln1_g, ln1_b, w_xq, w_xkv, w_xo, ln2_g, ln2_b, w_up, ffn_conv_w,
           ffn_conv_b, w_down, ln3_g, ln3_b):
    bp, sp, _ = x_prompt.shape
    nb, ts, _ = x_sample.shape
    assert bp == 1 and ts == 1
    depth = w_in.shape[0]
    assert depth == DEPTH
    win_caches = (cache_win1_kv, cache_win2_kv, cache_win3_kv)

    tables_p = _rope_tables(jnp.arange(sp))
    tables_s = _rope_tables(jnp.full((nb,), PAST_LEN, jnp.int32))
    mem_b = mem_prompt.reshape(N_MEM, D_MODEL).astype(BF16)

    bf = lambda a: a.astype(BF16)
    w_gates = bf(w_in[:, :, OFF_GATES:])
    w_attn_out, w_conv_out = bf(w_attn_out), bf(w_conv_out)
    w_xq, w_up, w_down = bf(w_xq), bf(w_up), bf(w_down)

    xp = x_prompt.reshape(sp, D_MODEL)
    xs = x_sample.reshape(nb, D_MODEL)
    xpb, xsb = bf(xp), bf(xs)
    tm_s = nb
    keep_rows = min(max(win for win, _ in DIL_GROUPS), sp)
    q_scale = HEAD_DIM ** -0.5
    xq_scale = X_HEAD_DIM ** -0.5

    win_p = [[] for _ in DIL_GROUPS]
    win_s = [[] for _ in DIL_GROUPS]
    conv_p, ffn_p, memkv_p, conv_s, ffn_s = [], [], [], [], []
    for l in range(depth):
        bg = b_gate[l].reshape(1, 2 * D_MODEL)

        mkv, mkv_b = _matmul(mem_b, w_xkv, l, [F32, BF16], N_MEM, "mem_kv")
        qs = _qkv_proj(xpb, w_in, l, OFF_Q, tables_p, True, q_scale, 512, True, [], "q_proj_p")
        *ks, k = _qkv_proj(xpb, w_in, l, OFF_K, tables_p, True, 1.0, 512, True, [F32], "k_proj_p", keep_rows)
        *vs, v = _qkv_proj(xpb, w_in, l, OFF_V, tables_p, False, 1.0, 512, True, [F32], "v_proj_p", keep_rows)
        u = _glu_proj(xpb, w_in, l, 256, "glu_proj_p")
        gates = _gate_proj(xpb, w_gates, l, bg, 2048, "gate_proj_p")
        parts = []
        for g in range(N_GROUPS):
            parts.extend(_win_attn_prompt(qs[g], ks[g], vs[g], g))
        attn = _combine_groups(parts, 512)
        cact = _conv_prompt(u, conv_w[l], conv_b[l], conv_ln_g[l], conv_ln_b[l])
        mix = _mix(attn, cact, gates, w_attn_out, w_conv_out, l, 1024)
        xp, xpb = _res_ln(mix, w_mix_out, l, xp, ln1_g[l], ln1_b[l], 256, "mix_out_ln1")
        (xq,) = _matmul(xpb, w_xq, l, [BF16], 2048, "xq_p", scale=xq_scale)
        xo = _xattn_prompt(xq, mkv_b, 512)
        xp, xpb = _res_ln(xo, w_xo, l, xp, ln2_g[l], ln2_b[l], 256, "xo_ln2")
        f, h_last = _ffn_prompt(xpb, w_up, l, ffn_conv_w[l], ffn_conv_b[l], 1024)
        xp, xpb = _res_ln(f, w_down, l, xp, ln3_g[l], ln3_b[l], 256, "down_ln3")
        for g, (win, _) in enumerate(DIL_GROUPS):
            win_p[g].append(_window_rows(k, v, g, min(win, sp))[None])
        conv_p.append(u[None, sp - (CONV_K - 1):])
        ffn_p.append(h_last[None])
        memkv_p.append(mkv.reshape(1, N_MEM, 2, X_HEADS, X_HEAD_DIM))

        (q,) = _qkv_proj(xsb, w_in, l, OFF_Q, tables_s, True, q_scale, tm_s, False, [F32], "q_proj_s")
        (k,) = _qkv_proj(xsb, w_in, l, OFF_K, tables_s, True, 1.0, tm_s, False, [F32], "k_proj_s")
        (v,) = _qkv_proj(xsb, w_in, l, OFF_V, tables_s, False, 1.0, tm_s, False, [F32], "v_proj_s")
        u = _glu_proj(xsb, w_in, l, tm_s, "glu_proj_s")
        gates = _gate_proj(xsb, w_gates, l, bg, tm_s, "gate_proj_s")
        attn = bf(_win_attn_sample(q, k, v, win_caches, l))
        cact, new_conv = _conv_sample(state_conv, u, conv_w[l], conv_b[l], conv_ln_g[l], conv_ln_b[l], l)
        mix = _mix(attn, cact, gates, w_attn_out, w_conv_out, l, tm_s)
        xs, xsb = _res_ln(mix, w_mix_out, l, xs, ln1_g[l], ln1_b[l], tm_s, "mix_out_ln1")
        (xq,) = _matmul(xsb, w_xq, l, [F32], tm_s, "xq_s", scale=xq_scale)
        xo = bf(_xattn_sample(xq, cache_mem_kv, l))
        xs, xsb = _res_ln(xo, w_xo, l, xs, ln2_g[l], ln2_b[l], tm_s, "xo_ln2")
        f, h_new = _ffn_sample(xsb, w_up, state_ffn_conv, ffn_conv_w[l], ffn_conv_b[l], l)
        xs, xsb = _res_ln(f, w_down, l, xs, ln3_g[l], ln3_b[l], tm_s, "down_ln3")
        for g in range(N_GROUPS):
            win_s[g].append(_window_rows(k, v, g, nb).reshape(nb, 1, 2, HEADS_PER_GROUP, HEAD_DIM))
        conv_s.append(new_conv)
        ffn_s.append(jnp.concatenate([state_ffn_conv[l][:, 1:], h_new[:, None]], axis=1))

    return (xp.reshape(bp, sp, D_MODEL), xs.reshape(nb, ts, D_MODEL),
            jnp.stack(win_p[0]), jnp.stack(win_p[1]), jnp.stack(win_p[2]), jnp.stack(conv_p),
            jnp.stack(ffn_p), jnp.stack(memkv_p),
            jnp.stack(win_s[0]), jnp.stack(win_s[1]), jnp.stack(win_s[2]), jnp.stack(conv_s),
            jnp.stack(ffn_s))
```

```python
import functools

import jax
import jax.numpy as jnp
from jax import lax
from jax.experimental import pallas as pl
from jax.experimental.pallas import tpu as pltpu

F32 = jnp.float32
BF16 = jnp.bfloat16

D_MODEL = 2048
PAST_LEN = 2048
HEAD_DIM = 128
HEADS_PER_GROUP = 4
DIL_GROUPS = ((128, 1), (512, 4), (2048, 16))
DILATIONS = tuple(d for _, d in DIL_GROUPS)
N_GROUPS = len(DIL_GROUPS)
ATTN_HEADS = N_GROUPS * HEADS_PER_GROUP
ATTN_WIDTH = ATTN_HEADS * HEAD_DIM
SLOT_WIDTH = HEADS_PER_GROUP * HEAD_DIM
ROT_DIM = HEAD_DIM // 4
ROT_HALF = ROT_DIM // 2
ROPE_THETA = 500000.0
CONV_WIDTH = 3 * D_MODEL // 4
CONV_K = 31
N_MEM = 256
X_HEADS = 4
X_HEAD_DIM = D_MODEL // X_HEADS
D_FF = 11 * D_MODEL // 4
FFN_K = 3
LN_EPS = 1e-5
DEPTH = 2
DN_ALPHA = (2 * DEPTH) ** 0.25

OFF_Q = 0
OFF_K = ATTN_WIDTH
OFF_V = 2 * ATTN_WIDTH
OFF_A = 3 * ATTN_WIDTH
OFF_BG = OFF_A + CONV_WIDTH
OFF_GATES = OFF_BG + CONV_WIDTH

LANES = 128
SUBLANES = 8
MXU_COLS = 256
KEYS_PER_GROUP = 128
VMEM_CAP = 56 * 1024 * 1024
MASK_VALUE = -1e30
TN = 512
CAST_ROWS = 256


def _params(n_axes, vmem_bytes):
    return pltpu.CompilerParams(dimension_semantics=("arbitrary",) * n_axes,
                                vmem_limit_bytes=int(min(vmem_bytes, VMEM_CAP)))


def _layer_norm(x, g, b):
    mu = jnp.mean(x, axis=-1, keepdims=True)
    xc = x - mu
    var = jnp.mean(xc * xc, axis=-1, keepdims=True)
    return xc * lax.rsqrt(var + LN_EPS) * g + b


def _dot(a, b):
    return jnp.dot(a, b, preferred_element_type=F32)


def _dot_nt(a, b):
    return lax.dot_general(a, b, (((1,), (1,)), ((), ())), preferred_element_type=F32)


def _w_spec(k, layer, off=0):
    return pl.BlockSpec((None, k, TN), lambda i, j: (layer, 0, j + off))


def _resident_w_spec(k, n, layer, col_block):
    return pl.BlockSpec((None, k, n), lambda i: (layer, 0, col_block), pipeline_mode=pl.Buffered(1))


def _cast_weight_once(w_ref, wb_ref):
    @pl.when(pl.program_id(0) == 0)
    def _():
        def rows(i, carry):
            r = pl.ds(pl.multiple_of(i * CAST_ROWS, CAST_ROWS), CAST_ROWS)
            wb_ref[r, :] = w_ref[r, :].astype(wb_ref.dtype)
            return carry
        lax.fori_loop(0, w_ref.shape[0] // CAST_ROWS, rows, 0)


def _rope_head(xh, c, s1, s2, scale):
    r = xh * c + pltpu.roll(xh, HEAD_DIM - ROT_HALF, 1) * s1 + pltpu.roll(xh, ROT_HALF, 1) * s2
    return r * scale if scale != 1.0 else r


def _qkv_kernel(x_ref, w_ref, c_ref, s1_ref, s2_ref, *refs, rope, scale, n_perm, n_plain):
    perm_refs = refs[:n_perm]
    plain_refs = refs[n_perm:n_perm + n_plain]
    wb_ref, scr = refs[n_perm + n_plain:]
    _cast_weight_once(w_ref, wb_ref)
    x = x_ref[...]
    tm = x.shape[0]
    heads_per_chunk = MXU_COLS // HEAD_DIM
    if rope:
        c, s1, s2 = c_ref[...], s1_ref[...], s2_ref[...]
    for chunk in range(ATTN_WIDTH // MXU_COLS):
        acc = _dot(x, wb_ref[:, chunk * MXU_COLS:(chunk + 1) * MXU_COLS])
        for hc in range(heads_per_chunk):
            head = chunk * heads_per_chunk + hc
            xh = acc[:, hc * HEAD_DIM:(hc + 1) * HEAD_DIM]
            if rope:
                xh = _rope_head(xh, c, s1, s2, scale)
            sl = slice(head * HEAD_DIM, (head + 1) * HEAD_DIM)
            for o in plain_refs:
                o[:, sl] = xh.astype(o.dtype)
            if n_perm:
                scr[head] = xh
    for g in range(n_perm):
        dil = DILATIONS[g]
        o = perm_refs[g]
        for h in range(HEADS_PER_GROUP):
            head = g * HEADS_PER_GROUP + h
            sl = slice(h * HEAD_DIM, (h + 1) * HEAD_DIM)
            for r in range(dil):
                rows = scr[head] if dil == 1 else scr[head, pl.ds(r, tm // dil, stride=dil), :]
                o[r, :, sl] = rows.astype(o.dtype)


def _qkv_proj(x, w_in, layer, col_off, tables, rope, scale, tm, perm, plain_dtypes, name, plain_rows=None):
    m, k = x.shape
    n_perm = N_GROUPS if perm else 0
    tab_spec = pl.BlockSpec((tm, HEAD_DIM), lambda i: (i, 0))
    perm_shapes = [jax.ShapeDtypeStruct((d, m // d, SLOT_WIDTH), BF16) for d in DILATIONS[:n_perm]]
    perm_specs = [pl.BlockSpec((d, tm // d, SLOT_WIDTH), lambda i: (0, i, 0)) for d in DILATIONS[:n_perm]]
    plain_rows = plain_rows or m
    skipped = (m - plain_rows) // tm
    plain_shapes = [jax.ShapeDtypeStruct((plain_rows, ATTN_WIDTH), d) for d in plain_dtypes]
    plain_specs = [pl.BlockSpec((tm, ATTN_WIDTH), lambda i: (jnp.maximum(i - skipped, 0), 0))
                   for _ in plain_dtypes]
    vmem = (2 * tm * k * 2 + k * ATTN_WIDTH * 6 + 2 * 3 * tm * HEAD_DIM * 4 + tm * ATTN_WIDTH * 4
            + 2 * n_perm * tm * SLOT_WIDTH * 2
            + 2 * sum(tm * ATTN_WIDTH * jnp.dtype(d).itemsize for d in plain_dtypes)
            + tm * MXU_COLS * 32 + (6 << 20))
    return pl.pallas_call(
        functools.partial(_qkv_kernel, rope=rope, scale=scale, n_perm=n_perm, n_plain=len(plain_dtypes)),
        out_shape=perm_shapes + plain_shapes,
        grid=(m // tm,),
        in_specs=[pl.BlockSpec((tm, k), lambda i: (i, 0)),
                  _resident_w_spec(k, ATTN_WIDTH, layer, col_off // ATTN_WIDTH),
                  tab_spec, tab_spec, tab_spec],
        out_specs=perm_specs + plain_specs,
        scratch_shapes=[pltpu.VMEM((k, ATTN_WIDTH), BF16), pltpu.VMEM((ATTN_HEADS, tm, HEAD_DIM), F32)],
        compiler_params=_params(1, vmem),
        name=name,
    )(x, w_in, *tables)


def _glu_kernel(x_ref, wa_ref, wb_ref, o_ref, wa_scr, wb_scr):
    _cast_weight_once(wa_ref, wa_scr)
    _cast_weight_once(wb_ref, wb_scr)
    x = x_ref[...]
    for chunk in range(CONV_WIDTH // MXU_COLS):
        cs = slice(chunk * MXU_COLS, (chunk + 1) * MXU_COLS)
        o_ref[:, cs] = _dot(x, wa_scr[:, cs]) * jax.nn.sigmoid(_dot(x, wb_scr[:, cs]))


def _glu_proj(x, w_in, layer, tm, name):
    m, k = x.shape
    vmem = 2 * tm * k * 2 + 2 * k * CONV_WIDTH * 6 + 2 * tm * CONV_WIDTH * 4 + tm * MXU_COLS * 32 + (4 << 20)
    return pl.pallas_call(
        _glu_kernel,
        out_shape=jax.ShapeDtypeStruct((m, CONV_WIDTH), F32),
        grid=(m // tm,),
        in_specs=[pl.BlockSpec((tm, k), lambda i: (i, 0)),
                  _resident_w_spec(k, CONV_WIDTH, layer, OFF_A // CONV_WIDTH),
                  _resident_w_spec(k, CONV_WIDTH, layer, OFF_BG // CONV_WIDTH)],
        out_specs=pl.BlockSpec((tm, CONV_WIDTH), lambda i: (i, 0)),
        scratch_shapes=[pltpu.VMEM((k, CONV_WIDTH), BF16)] * 2,
        compiler_params=_params(1, vmem),
        name=name,
    )(x, w_in, w_in)


def _gate_kernel(x_ref, w_ref, b_ref, o_ref):
    x = x_ref[...]
    for chunk in range(TN // MXU_COLS):
        cs = slice(chunk * MXU_COLS, (chunk + 1) * MXU_COLS)
        o_ref[:, cs] = jax.nn.sigmoid(_dot(x, w_ref[:, cs]) + b_ref[:, cs]).astype(o_ref.dtype)


def _gate_proj(x, w_gates, layer, b_gate, tm, name):
    m, k = x.shape
    vmem = 2 * (tm * k * 2 + k * TN * 2 + tm * TN * 2) + tm * TN * 16
    return pl.pallas_call(
        _gate_kernel,
        out_shape=jax.ShapeDtypeStruct((m, 2 * D_MODEL), BF16),
        grid=(m // tm, 2 * D_MODEL // TN),
        in_specs=[pl.BlockSpec((tm, k), lambda i, j: (i, 0)), _w_spec(k, layer),
                  pl.BlockSpec((1, TN), lambda i, j: (0, j))],
        out_specs=pl.BlockSpec((tm, TN), lambda i, j: (i, j)),
        compiler_params=_params(2, vmem + (8 << 20)),
        name=name,
    )(x, w_gates, b_gate)


def _mm_kernel(x_ref, w_ref, *out_refs, scale):
    acc = _dot(x_ref[...], w_ref[...].astype(BF16))
    if scale != 1.0:
        acc = acc * scale
    for o in out_refs:
        o[...] = acc.astype(o.dtype)


def _matmul(x, w, layer, out_dtypes, tm, name, scale=1.0):
    m, k = x.shape
    n_cols = w.shape[2]
    vmem = 2 * (tm * k * 2 + k * TN * jnp.dtype(w.dtype).itemsize
                + sum(tm * TN * jnp.dtype(d).itemsize for d in out_dtypes)) + tm * TN * 8 + k * TN * 2
    return pl.pallas_call(
        functools.partial(_mm_kernel, scale=scale),
        out_shape=[jax.ShapeDtypeStruct((m, n_cols), d) for d in out_dtypes],
        grid=(m // tm, n_cols // TN),
        in_specs=[pl.BlockSpec((tm, k), lambda i, j: (i, 0)), _w_spec(k, layer)],
        out_specs=[pl.BlockSpec((tm, TN), lambda i, j: (i, j)) for _ in out_dtypes],
        compiler_params=_params(2, vmem + (8 << 20)),
        name=name,
    )(x, w)


WIN_BLOCKS_PER_STEP = 4


def _win_attn_kernel(q_ref, kc_ref, kp_ref, vc_ref, vp_ref, o_ref, l_ref):
    i = pl.program_id(1)
    tq = KEYS_PER_GROUP
    row = lax.broadcasted_iota(jnp.int32, (tq, tq), 0)
    col = lax.broadcasted_iota(jnp.int32, (tq, tq), 1)
    mask_c = col <= row
    mask_p = col >= row
    mask_first = jnp.logical_and(mask_p, i > 0)
    for blk in range(q_ref.shape[0] // tq):
        rows = slice(blk * tq, (blk + 1) * tq)
        for h in range(HEADS_PER_GROUP):
            sl = slice(h * HEAD_DIM, (h + 1) * HEAD_DIM)
            q = q_ref[rows, sl]
            if blk == 0:
                k_prev, v_prev, mask_prev = kp_ref[:, sl], vp_ref[:, sl], mask_first
            else:
                before = slice((blk - 1) * tq, blk * tq)
                k_prev, v_prev, mask_prev = kc_ref[before, sl], vc_ref[before, sl], mask_p
            sc = jnp.where(mask_c, _dot_nt(q, kc_ref[rows, sl]), MASK_VALUE)
            sp = jnp.where(mask_prev, _dot_nt(q, k_prev), MASK_VALUE)
            m = jnp.maximum(jnp.max(sc, axis=-1, keepdims=True), jnp.max(sp, axis=-1, keepdims=True))
            pc = jnp.exp(sc - m)
            pp = jnp.exp(sp - m)
            den = jnp.sum(pc, axis=-1, keepdims=True) + jnp.sum(pp, axis=-1, keepdims=True)
            o = _dot(pc.astype(BF16), vc_ref[rows, sl]) + _dot(pp.astype(BF16), v_prev)
            o_ref[rows, sl] = o * (1.0 / den)
            l_ref[rows, sl] = jnp.broadcast_to(m + jnp.log(den), (tq, HEAD_DIM))


def _win_attn_prompt(q, k, v, g):
    dil, tc, _ = q.shape
    tq = WIN_BLOCKS_PER_STEP * KEYS_PER_GROUP
    cur = pl.BlockSpec((None, tq, SLOT_WIDTH), lambda r, i: (r, i, 0))
    prev = pl.BlockSpec((None, KEYS_PER_GROUP, SLOT_WIDTH),
                        lambda r, i: (r, jnp.maximum(i * WIN_BLOCKS_PER_STEP - 1, 0), 0))
    return pl.pallas_call(
        _win_attn_kernel,
        out_shape=[jax.ShapeDtypeStruct((dil, tc, SLOT_WIDTH), F32)] * 2,
        grid=(dil, tc // tq),
        in_specs=[cur, cur, prev, cur, prev],
        out_specs=[cur, cur],
        compiler_params=_params(2, 24 << 20),
        name=f"win_attn_g{g}",
    )(q, k, k, v, v)


def _combine_kernel(o1, l1, o2, l2, o3, l3, out_ref, so2, sl2, so3, sl3):
    tm = out_ref.shape[0]
    lane_blocks = SLOT_WIDTH // LANES
    for src, dst, dil in ((o2, so2, DILATIONS[1]), (l2, sl2, DILATIONS[1]),
                          (o3, so3, DILATIONS[2]), (l3, sl3, DILATIONS[2])):
        for c in range(lane_blocks):
            for r in range(dil):
                dst[c, pl.ds(r, tm // dil, stride=dil), :] = src[r, :, c * LANES:(c + 1) * LANES]
    for c in range(lane_blocks):
        ls = slice(c * LANES, (c + 1) * LANES)
        la, lb, lc = l1[0, :, ls], sl2[c], sl3[c]
        m = jnp.maximum(jnp.maximum(la, lb), lc)
        ea, eb, ec = jnp.exp(la - m), jnp.exp(lb - m), jnp.exp(lc - m)
        num = ea * o1[0, :, ls] + eb * so2[c] + ec * so3[c]
        out_ref[:, ls] = (num * (1.0 / (ea + eb + ec))).astype(out_ref.dtype)


def _combine_groups(parts, tm):
    t = parts[0].shape[1]
    specs = []
    for dil in DILATIONS:
        specs += [pl.BlockSpec((dil, tm // dil, SLOT_WIDTH), lambda i: (0, i, 0))] * 2
    return pl.pallas_call(
        _combine_kernel,
        out_shape=jax.ShapeDtypeStruct((t, SLOT_WIDTH), BF16),
        grid=(t // tm,),
        in_specs=specs,
        out_specs=pl.BlockSpec((tm, SLOT_WIDTH), lambda i: (i, 0)),
        scratch_shapes=[pltpu.VMEM((SLOT_WIDTH // LANES, tm, LANES), F32)] * 4,
        compiler_params=_params(1, 32 << 20),
        name="combine_groups",
    )(*parts)


KV_ROWS = 2 * HEADS_PER_GROUP


def _win_decode_kernel(q_ref, kvn_ref, c1_ref, c2_ref, c3_ref, o_ref):
    caches = (c1_ref, c2_ref, c3_ref)
    bs = q_ref.shape[0]

    def body(b, carry):
        q8 = q_ref[b]
        scores = [jnp.sum(kvn_ref[b] * q8, axis=-1, keepdims=True)]
        for g in range(N_GROUPS):
            scores.append(jnp.sum(caches[g][b] * q8[g][None], axis=-1, keepdims=True))
        m = jnp.max(scores[0], axis=0)
        for s in scores[1:]:
            m = jnp.maximum(m, jnp.max(s, axis=0))
        num = jnp.zeros((KV_ROWS, HEAD_DIM), F32)
        den = jnp.zeros((KV_ROWS, 1), F32)
        for s, rows in zip(scores, [kvn_ref[b]] + [c[b] for c in caches]):
            p = jnp.exp(s - m[None])
            den = den + jnp.sum(p, axis=0)
            p_val = pltpu.roll(jnp.broadcast_to(p, rows.shape), HEADS_PER_GROUP, 1)
            num = num + jnp.sum(p_val * rows, axis=0)
        o_ref[b] = num * (1.0 / pltpu.roll(den, HEADS_PER_GROUP, 0))
        return carry

    lax.fori_loop(0, bs, body, 0, unroll=2)


def _win_attn_sample(q, k_new, v_new, caches, layer):
    b = q.shape[0]
    bs = 4
    views, specs = [], []
    for cache, (win, dil) in zip(caches, DIL_GROUPS):
        depth, _, n_past = cache.shape[:3]
        assert n_past == win and n_past % dil == 0
        views.append(cache.reshape(depth, b, n_past // dil, dil, KV_ROWS, HEAD_DIM))
        specs.append(pl.BlockSpec((None, bs, KEYS_PER_GROUP, None, KV_ROWS, HEAD_DIM),
                                  lambda i: (layer, i, 0, 0, 0, 0)))
    heads = lambda a: a.reshape(b, N_GROUPS, HEADS_PER_GROUP, HEAD_DIM)
    q8 = jnp.concatenate([heads(q), heads(q)], axis=2)
    kv8 = jnp.concatenate([heads(k_new), heads(v_new)], axis=2)
    row_spec = pl.BlockSpec((bs, N_GROUPS, KV_ROWS, HEAD_DIM), lambda i: (i, 0, 0, 0))
    vmem = 2 * (3 * bs * KEYS_PER_GROUP * KV_ROWS * HEAD_DIM * 4) + (16 << 20)
    out = pl.pallas_call(
        _win_decode_kernel,
        out_shape=jax.ShapeDtypeStruct((b, KV_ROWS, HEAD_DIM), F32),
        grid=(b // bs,),
        in_specs=[row_spec, row_spec] + specs,
        out_specs=pl.BlockSpec((bs, KV_ROWS, HEAD_DIM), lambda i: (i, 0, 0)),
        compiler_params=_params(1, vmem),
        name="win_attn_sample",
    )(q8, kv8, *views)
    return out[:, HEADS_PER_GROUP:].reshape(b, SLOT_WIDTH)


CONV_HALO = 32
CONV_ROWS = 64
LN_ROWS = 16


def _conv_prompt_kernel(u_ref, halo_ref, w_ref, b_ref, g_ref, beta_ref, o_ref, ext_ref, sh_ref, c_ref):
    i = pl.program_id(0)
    tt = u_ref.shape[0]
    first_tap = CONV_HALO - (CONV_K - 1)
    halo = halo_ref[...]
    ext_ref[0:CONV_HALO, :] = jnp.where(i > 0, halo, jnp.zeros_like(halo))
    ext_ref[CONV_HALO:, :] = u_ref[...]
    n_sh = tt + CONV_HALO - SUBLANES
    for s in range(1, SUBLANES):
        sh_ref[s - 1] = ext_ref[pl.ds(s, n_sh), :]

    n_tiles = CONV_ROWS // SUBLANES

    def tap_rows(ci, carry):
        r0 = pl.multiple_of(ci * CONV_ROWS, CONV_ROWS)
        for lb in range(CONV_WIDTH // LANES):
            ls = slice(lb * LANES, (lb + 1) * LANES)
            bias = jnp.broadcast_to(b_ref[:, ls], (SUBLANES, LANES))
            acc = [bias] * n_tiles
            for s in range(SUBLANES):
                taps = [(a, a * SUBLANES + s - first_tap) for a in range((CONV_HALO + SUBLANES) // SUBLANES)
                        if 0 <= a * SUBLANES + s - first_tap < CONV_K]
                a_lo, a_hi = taps[0][0], taps[-1][0]
                rows = pl.ds(r0 + a_lo * SUBLANES, CONV_ROWS + (a_hi - a_lo) * SUBLANES)
                strip = ext_ref[rows, ls] if s == 0 else sh_ref[s - 1, rows, ls]
                for a, j in taps:
                    wj = w_ref[j:j + 1, ls]
                    for k in range(n_tiles):
                        t0 = (k + a - a_lo) * SUBLANES
                        acc[k] = acc[k] + strip[t0:t0 + SUBLANES] * wj
            c_ref[pl.ds(r0, CONV_ROWS), ls] = jnp.concatenate(acc, axis=0)
        return carry

    lax.fori_loop(0, tt // CONV_ROWS, tap_rows, 0)

    def ln_rows(ci, carry):
        r0 = pl.multiple_of(ci * LN_ROWS, LN_ROWS)
        y = _layer_norm(c_ref[pl.ds(r0, LN_ROWS), :], g_ref[...], beta_ref[...])
        o_ref[pl.ds(r0, LN_ROWS), :] = (y * jax.nn.sigmoid(y)).astype(o_ref.dtype)
        return carry

    lax.fori_loop(0, tt // LN_ROWS, ln_rows, 0, unroll=4)


def _conv_prompt(u, conv_w, conv_b, ln_g, ln_b):
    t = u.shape[0]
    tt = 256
    row = lambda a: a.reshape(1, CONV_WIDTH)
    full = lambda shape: pl.BlockSpec(shape, lambda i: (0, 0))
    hb = tt // CONV_HALO
    vmem = (2 * (tt + CONV_HALO) * CONV_WIDTH * 4 + 2 * tt * CONV_WIDTH * 2
            + (8 * (tt + CONV_HALO) + tt) * CONV_WIDTH * 4 + (8 << 20))
    return pl.pallas_call(
        _conv_prompt_kernel,
        out_shape=jax.ShapeDtypeStruct((t, CONV_WIDTH), BF16),
        grid=(t // tt,),
        in_specs=[pl.BlockSpec((tt, CONV_WIDTH), lambda i: (i, 0)),
                  pl.BlockSpec((CONV_HALO, CONV_WIDTH), lambda i: (jnp.maximum(i * hb - 1, 0), 0)),
                  full((CONV_K, CONV_WIDTH)), full((1, CONV_WIDTH)),
                  full((1, CONV_WIDTH)), full((1, CONV_WIDTH))],
        out_specs=pl.BlockSpec((tt, CONV_WIDTH), lambda i: (i, 0)),
        scratch_shapes=[pltpu.VMEM((tt + CONV_HALO, CONV_WIDTH), F32),
                        pltpu.VMEM((SUBLANES - 1, tt + CONV_HALO - SUBLANES, CONV_WIDTH), F32),
                        pltpu.VMEM((tt, CONV_WIDTH), F32)],
        compiler_params=_params(1, vmem),
        name="conv_prompt",
    )(u, u, conv_w, row(conv_b), row(ln_g), row(ln_b))


def _conv_sample_kernel(st_ref, u_ref, w_ref, b_ref, g_ref, beta_ref, o_ref, ns_ref, c_ref):
    bs = u_ref.shape[0]
    hist = CONV_K - 1
    w_hist = w_ref[0:hist, :]
    for b in range(bs):
        st = st_ref[b]
        c_ref[b:b + 1, :] = jnp.sum(st * w_hist, axis=0, keepdims=True)
        ns_ref[b, 0:hist - 1, :] = st[1:hist]
        ns_ref[b, hist - 1:hist, :] = u_ref[b:b + 1, :]
    acc = c_ref[...] + u_ref[...] * w_ref[CONV_K - 1:CONV_K, :] + b_ref[...]
    y = _layer_norm(acc, g_ref[...], beta_ref[...])
    o_ref[...] = (y * jax.nn.sigmoid(y)).astype(o_ref.dtype)


def _conv_sample(state, u, conv_w, conv_b, ln_g, ln_b, layer):
    b = u.shape[0]
    bs = 16
    row = lambda a: a.reshape(1, CONV_WIDTH)
    full = lambda shape: pl.BlockSpec(shape, lambda i: (0, 0))
    return pl.pallas_call(
        _conv_sample_kernel,
        out_shape=[jax.ShapeDtypeStruct((b, CONV_WIDTH), BF16),
                   jax.ShapeDtypeStruct((b, CONV_K - 1, CONV_WIDTH), F32)],
        grid=(b // bs,),
        in_specs=[pl.BlockSpec((None, bs, CONV_K - 1, CONV_WIDTH), lambda i: (layer, i, 0, 0)),
                  pl.BlockSpec((bs, CONV_WIDTH), lambda i: (i, 0)),
                  full((CONV_K, CONV_WIDTH)), full((1, CONV_WIDTH)),
                  full((1, CONV_WIDTH)), full((1, CONV_WIDTH))],
        out_specs=[pl.BlockSpec((bs, CONV_WIDTH), lambda i: (i, 0)),
                   pl.BlockSpec((bs, CONV_K - 1, CONV_WIDTH), lambda i: (i, 0, 0))],
        scratch_shapes=[pltpu.VMEM((bs, CONV_WIDTH), F32)],
        compiler_params=_params(1, 4 * bs * 32 * CONV_WIDTH * 4 + (8 << 20)),
        name="conv_sample",
    )(state, u, conv_w, row(conv_b), row(ln_g), row(ln_b))


def _mix_kernel(a_ref, c_ref, wa_ref, wc_ref, ga_ref, gc_ref, o_ref):
    a = _dot(a_ref[...], wa_ref[...])
    c = _dot(c_ref[...], wc_ref[...])
    o_ref[...] = (ga_ref[...].astype(F32) * a + gc_ref[...].astype(F32) * c).astype(o_ref.dtype)


def _mix(attn, cact, gates, w_attn_out, w_conv_out, layer, tm):
    m = attn.shape[0]
    ng = D_MODEL // TN
    tile = pl.BlockSpec((tm, TN), lambda i, j: (i, j))
    vmem = 2 * (tm * (SLOT_WIDTH + CONV_WIDTH) * 2 + (SLOT_WIDTH + CONV_WIDTH) * TN * 2
                + 3 * tm * TN * 2) + tm * TN * 16 + (8 << 20)
    return pl.pallas_call(
        _mix_kernel,
        out_shape=jax.ShapeDtypeStruct((m, D_MODEL), BF16),
        grid=(m // tm, D_MODEL // TN),
        in_specs=[pl.BlockSpec((tm, SLOT_WIDTH), lambda i, j: (i, 0)),
                  pl.BlockSpec((tm, CONV_WIDTH), lambda i, j: (i, 0)),
                  _w_spec(SLOT_WIDTH, layer), _w_spec(CONV_WIDTH, layer), tile,
                  pl.BlockSpec((tm, TN), lambda i, j: (i, j + ng))],
        out_specs=tile,
        compiler_params=_params(2, vmem),
        name="mix",
    )(attn, cact, w_attn_out, w_conv_out, gates, gates)


def _res_ln_kernel(h_ref, w_ref, x_ref, g_ref, b_ref, o_ref, ob_ref, *scratch):
    if scratch:
        _cast_weight_once(w_ref, scratch[0])
        w = scratch[0][...]
    else:
        w = w_ref[...]
    y = DN_ALPHA * x_ref[...] + _dot(h_ref[...], w)
    y = _layer_norm(y, g_ref[...], b_ref[...])
    o_ref[...] = y
    ob_ref[...] = y.astype(ob_ref.dtype)


def _res_ln(h, w, layer, x, g, b, tm, name):
    m, k = h.shape
    cast = w.dtype != BF16
    full = lambda shape: pl.BlockSpec(shape, lambda i: (0, 0))
    w_bytes = k * D_MODEL * (6 if cast else 2)
    vmem = 2 * (tm * k * 2 + tm * D_MODEL * (4 + 4 + 2)) + w_bytes + tm * D_MODEL * 12 + (4 << 20)
    return pl.pallas_call(
        _res_ln_kernel,
        out_shape=[jax.ShapeDtypeStruct((m, D_MODEL), F32), jax.ShapeDtypeStruct((m, D_MODEL), BF16)],
        grid=(m // tm,),
        in_specs=[pl.BlockSpec((tm, k), lambda i: (i, 0)),
                  _resident_w_spec(k, D_MODEL, layer, 0),
                  pl.BlockSpec((tm, D_MODEL), lambda i: (i, 0)),
                  full((1, D_MODEL)), full((1, D_MODEL))],
        out_specs=[pl.BlockSpec((tm, D_MODEL), lambda i: (i, 0))] * 2,
        scratch_shapes=[pltpu.VMEM((k, D_MODEL), BF16)] if cast else [],
        compiler_params=_params(1, vmem),
        name=name,
    )(h, w, x, g.reshape(1, D_MODEL), b.reshape(1, D_MODEL))


def _xattn_prompt_kernel(q_ref, kv_ref, o_ref):
    for h in range(X_HEADS):
        sl = slice(h * X_HEAD_DIM, (h + 1) * X_HEAD_DIM)
        s = _dot_nt(q_ref[:, sl], kv_ref[:, sl])
        p = jnp.exp(s - jnp.max(s, axis=-1, keepdims=True))
        den = jnp.sum(p, axis=-1, keepdims=True)
        o = _dot(p.astype(BF16), kv_ref[:, D_MODEL + h * X_HEAD_DIM:D_MODEL + (h + 1) * X_HEAD_DIM])
        o_ref[:, sl] = (o * (1.0 / den)).astype(o_ref.dtype)


def _xattn_prompt(q, mem_kv, tm):
    t = q.shape[0]
    return pl.pallas_call(
        _xattn_prompt_kernel,
        out_shape=jax.ShapeDtypeStruct((t, D_MODEL), BF16),
        grid=(t // tm,),
        in_specs=[pl.BlockSpec((tm, D_MODEL), lambda i: (i, 0)),
                  pl.BlockSpec((N_MEM, 2 * D_MODEL), lambda i: (0, 0))],
        out_specs=pl.BlockSpec((tm, D_MODEL), lambda i: (i, 0)),
        compiler_params=_params(1, 32 << 20),
        name="xattn_prompt",
    )(q, mem_kv)


XATTN_KEY_CHUNK = 32


def _xattn_sample_kernel(q_ref, kv_ref, o_ref):
    bs = q_ref.shape[0]
    nl = X_HEAD_DIM // LANES
    qs = [q_ref[b] for b in range(bs)]

    def chunk(ci, carry):
        k0 = pl.multiple_of(ci * XATTN_KEY_CHUNK, XATTN_KEY_CHUNK)
        keys = pl.ds(k0, XATTN_KEY_CHUNK)
        out = []
        for b in range(bs):
            m, den, num = carry[b]
            q = qs[b]
            acc = kv_ref[b, keys, 0, :, 0:LANES] * q[None, :, 0:LANES]
            for c in range(1, nl):
                ls = slice(c * LANES, (c + 1) * LANES)
                acc = acc + kv_ref[b, keys, 0, :, ls] * q[None, :, ls]
            s = jnp.sum(acc, axis=-1, keepdims=True)
            m_new = jnp.maximum(m, jnp.max(s, axis=0))
            alpha = jnp.exp(m - m_new)
            p = jnp.exp(s - m_new[None])
            den = den * alpha + jnp.sum(p, axis=0)
            num = num * alpha + jnp.sum(p * kv_ref[b, keys, 1], axis=0)
            out.append((m_new, den, num))
        return tuple(out)

    init = tuple((jnp.full((X_HEADS, 1), MASK_VALUE, F32), jnp.zeros((X_HEADS, 1), F32),
                  jnp.zeros((X_HEADS, X_HEAD_DIM), F32)) for _ in range(bs))
    final = lax.fori_loop(0, N_MEM // XATTN_KEY_CHUNK, chunk, init)
    for b in range(bs):
        _, den, num = final[b]
        o_ref[b] = num * (1.0 / den)


def _xattn_sample(q, cache_mem_kv, layer):
    b = q.shape[0]
    bs = 4
    out = pl.pallas_call(
        _xattn_sample_kernel,
        out_shape=jax.ShapeDtypeStruct((b, X_HEADS, X_HEAD_DIM), F32),
        grid=(b // bs,),
        in_specs=[pl.BlockSpec((bs, X_HEADS, X_HEAD_DIM), lambda i: (i, 0, 0)),
                  pl.BlockSpec((None, bs, N_MEM, 2, X_HEADS, X_HEAD_DIM), lambda i: (layer, i, 0, 0, 0, 0))],
        out_specs=pl.BlockSpec((bs, X_HEADS, X_HEAD_DIM), lambda i: (i, 0, 0)),
        compiler_params=_params(1, 2 * 2 * bs * N_MEM * 2 * D_MODEL * 4 + (8 << 20)),
        name="xattn_sample",
    )(q.reshape(b, X_HEADS, X_HEAD_DIM), cache_mem_kv)
    return out.reshape(b, D_MODEL)


def _ffn_conv(h, tail, cw, cb):
    rows = lax.broadcasted_iota(jnp.int32, tail.shape, 0)
    out = cb + cw[FFN_K - 1:FFN_K] * h
    for back in range(1, FFN_K):
        shifted = pltpu.roll(h, back, 0)
        top = jnp.where(rows < back, pltpu.roll(tail, back, 0), shifted[0:SUBLANES])
        shifted = jnp.concatenate([top, shifted[SUBLANES:]], axis=0)
        out = out + cw[FFN_K - 1 - back:FFN_K - back] * shifted
    return out


def _ffn_prompt_kernel(x_ref, wg_ref, wv_ref, cwg_ref, cwv_ref, cbg_ref, cbv_ref, f_ref, lg_ref, lv_ref,
                       tg_ref, tv_ref):
    m = pl.program_id(1)
    tm = x_ref.shape[0]

    @pl.when(jnp.logical_and(pl.program_id(0) == 0, m == 0))
    def _():
        tg_ref[...] = jnp.zeros(tg_ref.shape, F32)
        tv_ref[...] = jnp.zeros(tv_ref.shape, F32)

    x = x_ref[...]
    tail_g = jnp.where(m > 0, tg_ref[...], 0.0)
    tail_v = jnp.where(m > 0, tv_ref[...], 0.0)
    new_tails, outs = [], []
    for chunk in range(TN // MXU_COLS):
        cs = slice(chunk * MXU_COLS, (chunk + 1) * MXU_COLS)
        hg = _dot(x, wg_ref[:, cs])
        hv = _dot(x, wv_ref[:, cs])
        gate = _ffn_conv(hg, tail_g[:, cs], cwg_ref[:, cs], cbg_ref[:, cs])
        val = _ffn_conv(hv, tail_v[:, cs], cwv_ref[:, cs], cbv_ref[:, cs])
        outs.append((gate * jax.nn.sigmoid(gate) * val).astype(f_ref.dtype))
        new_tails.append((hg[tm - SUBLANES:], hv[tm - SUBLANES:]))
    for chunk, (last_g, last_v) in enumerate(new_tails):
        cs = slice(chunk * MXU_COLS, (chunk + 1) * MXU_COLS)
        f_ref[:, cs] = outs[chunk]
        tg_ref[:, cs] = last_g
        tv_ref[:, cs] = last_v
        lg_ref[:, cs] = last_g
        lv_ref[:, cs] = last_v


def _ffn_prompt(x, w_up, layer, conv_w, conv_b, tm):
    t, k = x.shape
    nt = D_FF // TN
    cb = conv_b.reshape(1, 2 * D_FF)
    vmem = 2 * (tm * k * 2 + 2 * k * TN * 2 + tm * TN * 2) + tm * TN * 48 + (8 << 20)
    f, last_g, last_v = pl.pallas_call(
        _ffn_prompt_kernel,
        out_shape=[jax.ShapeDtypeStruct((t, D_FF), BF16),
                   jax.ShapeDtypeStruct((SUBLANES, D_FF), F32),
                   jax.ShapeDtypeStruct((SUBLANES, D_FF), F32)],
        grid=(nt, t // tm),
        in_specs=[pl.BlockSpec((tm, k), lambda n, m: (m, 0)),
                  pl.BlockSpec((None, k, TN), lambda n, m: (layer, 0, n)),
                  pl.BlockSpec((None, k, TN), lambda n, m: (layer, 0, n + nt)),
                  pl.BlockSpec((FFN_K, TN), lambda n, m: (0, n)),
                  pl.BlockSpec((FFN_K, TN), lambda n, m: (0, n + nt)),
                  pl.BlockSpec((1, TN), lambda n, m: (0, n)),
                  pl.BlockSpec((1, TN), lambda n, m: (0, n + nt))],
        out_specs=[pl.BlockSpec((tm, TN), lambda n, m: (m, n)),
                   pl.BlockSpec((SUBLANES, TN), lambda n, m: (0, n)),
                   pl.BlockSpec((SUBLANES, TN), lambda n, m: (0, n))],
        scratch_shapes=[pltpu.VMEM((SUBLANES, TN), F32)] * 2,
        compiler_params=_params(2, vmem),
        name="ffn_prompt",
    )(x, w_up, w_up, conv_w, conv_w, cb, cb)
    h_last = jnp.concatenate([last_g, last_v], axis=1)[SUBLANES - (FFN_K - 1):]
    return f, h_last


def _ffn_sample_kernel(x_ref, wg_ref, wv_ref, s0g_ref, s0v_ref, s1g_ref, s1v_ref,
                       cwg_ref, cwv_ref, cbg_ref, cbv_ref, f_ref, hg_ref, hv_ref):
    x = x_ref[...]

    def branch(w_ref, s0_ref, s1_ref, cw_ref, cb_ref, h_ref):
        h = _dot(x, w_ref[...])
        h_ref[...] = h
        return cb_ref[...] + cw_ref[0:1, :] * s0_ref[...] + cw_ref[1:2, :] * s1_ref[...] + cw_ref[2:3, :] * h

    gate = branch(wg_ref, s0g_ref, s1g_ref, cwg_ref, cbg_ref, hg_ref)
    val = branch(wv_ref, s0v_ref, s1v_ref, cwv_ref, cbv_ref, hv_ref)
    f_ref[...] = (gate * jax.nn.sigmoid(gate) * val).astype(f_ref.dtype)


def _ffn_sample(x, w_up, state, conv_w, conv_b, layer):
    b, k = x.shape
    nt = D_FF // TN
    st = state.reshape(state.shape[0], b, (FFN_K - 1) * 2 * D_FF)
    cb = conv_b.reshape(1, 2 * D_FF)
    st_spec = lambda off: pl.BlockSpec((None, b, TN), lambda n: (layer, 0, n + off))
    w_spec = lambda off: pl.BlockSpec((None, k, TN), lambda n: (layer, 0, n + off))
    col = lambda rows, off: pl.BlockSpec((rows, TN), lambda n: (0, n + off))
    f, hg, hv = pl.pallas_call(
        _ffn_sample_kernel,
        out_shape=[jax.ShapeDtypeStruct((b, D_FF), BF16),
                   jax.ShapeDtypeStruct((b, D_FF), F32), jax.ShapeDtypeStruct((b, D_FF), F32)],
        grid=(nt,),
        in_specs=[pl.BlockSpec((b, k), lambda n: (0, 0)),
                  w_spec(0), w_spec(nt),
                  st_spec(0), st_spec(nt), st_spec(2 * nt), st_spec(3 * nt),
                  col(FFN_K, 0), col(FFN_K, nt), col(1, 0), col(1, nt)],
        out_specs=[pl.BlockSpec((b, TN), lambda n: (0, n))] * 3,
        compiler_params=_params(1, 32 << 20),
        name="ffn_sample",
    )(x, w_up, w_up, st, st, st, st, conv_w, conv_w, cb, cb)
    return f, jnp.concatenate([hg, hv], axis=1)


def _rope_tables(pos):
    t = pos.shape[0]
    inv = ROPE_THETA ** (-jnp.arange(0, ROT_DIM, 2, dtype=F32) / ROT_DIM)
    ang = pos.astype(F32)[:, None] * inv[None, :]
    cos, sin = jnp.cos(ang), jnp.sin(ang)
    zeros = lambda n: jnp.zeros((t, n), F32)
    c = jnp.concatenate([cos, cos, jnp.ones((t, HEAD_DIM - ROT_DIM), F32)], axis=1)
    s1 = jnp.concatenate([-sin, zeros(HEAD_DIM - ROT_HALF)], axis=1)
    s2 = jnp.concatenate([zeros(ROT_HALF), sin, zeros(HEAD_DIM - ROT_DIM)], axis=1)
    return c, s1, s2


def _window_rows(k, v, g, rows):
    sl = slice(g * SLOT_WIDTH, (g + 1) * SLOT_WIDTH)
    kv = jnp.stack([k[k.shape[0] - rows:, sl], v[v.shape[0] - rows:, sl]], axis=1)
    return kv.reshape(rows, 2, HEADS_PER_GROUP, HEAD_DIM)


def kernel(x_prompt, x_sample, cache_win1_kv, cache_win2_kv, cache_win3_kv, state_conv, state_ffn_conv,
           cache_mem_kv, mem_prompt, w_in, b_gate, w_attn_out, conv_w, conv_b, conv_ln_g, conv_ln_b,
           w_conv_out, w_mix_out, ln1_g, ln1_b, w_xq, w_xkv, w_xo, ln2_g, ln2_b, w_up, ffn_conv_w,
           ffn_conv_b, w_down, ln3_g, ln3_b):
    bp, sp, _ = x_prompt.shape
    nb, ts, _ = x_sample.shape
    assert bp == 1 and ts == 1
    depth = w_in.shape[0]
    assert depth == DEPTH
    win_caches = (cache_win1_kv, cache_win2_kv, cache_win3_kv)

    tables_p = _rope_tables(jnp.arange(sp))
    tables_s = _rope_tables(jnp.full((nb,), PAST_LEN, jnp.int32))
    mem_b = mem_prompt.reshape(N_MEM, D_MODEL).astype(BF16)

    bf = lambda a: a.astype(BF16)
    w_gates = bf(w_in[:, :, OFF_GATES:])
    w_attn_out, w_conv_out = bf(w_attn_out), bf(w_conv_out)
    w_xq, w_up, w_down = bf(w_xq), bf(w_up), bf(w_down)

    xp = x_prompt.reshape(sp, D_MODEL)
    xs = x_sample.reshape(nb, D_MODEL)
    xpb, xsb = bf(xp), bf(xs)
    tm_s = nb
    keep_rows = min(max(win for win, _ in DIL_GROUPS), sp)
    q_scale = HEAD_DIM ** -0.5
    xq_scale = X_HEAD_DIM ** -0.5

    win_p = [[] for _ in DIL_GROUPS]
    win_s = [[] for _ in DIL_GROUPS]
    conv_p, ffn_p, memkv_p, conv_s, ffn_s = [], [], [], [], []
    for l in range(depth):
        bg = b_gate[l].reshape(1, 2 * D_MODEL)

        mkv, mkv_b = _matmul(mem_b, w_xkv, l, [F32, BF16], N_MEM, "mem_kv")
        qs = _qkv_proj(xpb, w_in, l, OFF_Q, tables_p, True, q_scale, 512, True, [], "q_proj_p")
        *ks, k = _qkv_proj(xpb, w_in, l, OFF_K, tables_p, True, 1.0, 512, True, [F32], "k_proj_p", keep_rows)
        *vs, v = _qkv_proj(xpb, w_in, l, OFF_V, tables_p, False, 1.0, 512, True, [F32], "v_proj_p", keep_rows)
        u = _glu_proj(xpb, w_in, l, 256, "glu_proj_p")
        gates = _gate_proj(xpb, w_gates, l, bg, 2048, "gate_proj_p")
        parts = []
        for g in range(N_GROUPS):
            parts.extend(_win_attn_prompt(qs[g], ks[g], vs[g], g))
        attn = _combine_groups(parts, 512)
        cact = _conv_prompt(u, conv_w[l], conv_b[l], conv_ln_g[l], conv_ln_b[l])
        mix = _mix(attn, cact, gates, w_attn_out, w_conv_out, l, 1024)
        xp, xpb = _res_ln(mix, w_mix_out, l, xp, ln1_g[l], ln1_b[l], 256, "mix_out_ln1")
        (xq,) = _matmul(xpb, w_xq, l, [BF16], 2048, "xq_p", scale=xq_scale)
        xo = _xattn_prompt(xq, mkv_b, 512)
        xp, xpb = _res_ln(xo, w_xo, l, xp, ln2_g[l], ln2_b[l], 256, "xo_ln2")
        f, h_last = _ffn_prompt(xpb, w_up, l, ffn_conv_w[l], ffn_conv_b[l], 1024)
        xp, xpb = _res_ln(f, w_down, l, xp, ln3_g[l], ln3_b[l], 256, "down_ln3")
        for g, (win, _) in enumerate(DIL_GROUPS):
            win_p[g].append(_window_rows(k, v, g, min(win, sp))[None])
        conv_p.append(u[None, sp - (CONV_K - 1):])
        ffn_p.append(h_last[None])
        memkv_p.append(mkv.reshape(1, N_MEM, 2, X_HEADS, X_HEAD_DIM))

        (q,) = _qkv_proj(xsb, w_in, l, OFF_Q, tables_s, True, q_scale, tm_s, False, [F32], "q_proj_s")
        (k,) = _qkv_proj(xsb, w_in, l, OFF_K, tables_s, True, 1.0, tm_s, False, [F32], "k_proj_s")
        (v,) = _qkv_proj(xsb, w_in, l, OFF_V, tables_s, False, 1.0, tm_s, False, [F32], "v_proj_s")
        u = _glu_proj(xsb, w_in, l, tm_s, "glu_proj_s")
        gates = _gate_proj(xsb, w_gates, l, bg, tm_s, "gate_proj_s")
        attn = bf(_win_attn_sample(q, k, v, win_caches, l))
        cact, new_conv = _conv_sample(state_conv, u, conv_w[l], conv_b[l], conv_ln_g[l], conv_ln_b[l], l)
        mix = _mix(attn, cact, gates, w_attn_out, w_conv_out, l, tm_s)
        xs, xsb = _res_ln(mix, w_mix_out, l, xs, ln1_g[l], ln1_b[l], tm_s, "mix_out_ln1")
        (xq,) = _matmul(xsb, w_xq, l, [F32], tm_s, "xq_s", scale=xq_scale)
        xo = bf(_xattn_sample(xq, cache_mem_kv, l))
        xs, xsb = _res_ln(xo, w_xo, l, xs, ln2_g[l], ln2_b[l], tm_s, "xo_ln2")
        f, h_new = _ffn_sample(xsb, w_up, state_ffn_conv, ffn_conv_w[l], ffn_conv_b[l], l)
        xs, xsb = _res_ln(f, w_down, l, xs, ln3_g[l], ln3_b[l], tm_s, "down_ln3")
        for g in range(N_GROUPS):
            win_s[g].append(_window_rows(k, v, g, nb).reshape(nb, 1, 2, HEADS_PER_GROUP, HEAD_DIM))
        conv_s.append(new_conv)
        ffn_s.append(jnp.concatenate([state_ffn_conv[l][:, 1:], h_new[:, None]], axis=1))

    return (xp.reshape(bp, sp, D_MODEL), xs.reshape(nb, ts, D_MODEL),
            jnp.stack(win_p[0]), jnp.stack(win_p[1]), jnp.stack(win_p[2]), jnp.stack(conv_p),
            jnp.stack(ffn_p), jnp.stack(memkv_p),
            jnp.stack(win_s[0]), jnp.stack(win_s[1]), jnp.stack(win_s[2]), jnp.stack(conv_s),
            jnp.stack(ffn_s))
```
